```python
import math
import jax, jax.numpy as jnp
from jax import lax
import numpy as np

D_MODEL = 1024
BATCH = 2
SEQ = 8192
DEPTH = 2

HEAD_DIM = 64
MLA_HEADS = 6
MLA_NOPE = 64
MLA_ROPE = 32
MLA_V = 64
MLA_QK = MLA_NOPE + MLA_ROPE
Q_LORA = 256
KV_LORA = 128
SWA_HEADS = 6
SWA_KV_HEADS = 2
SWA_GROUP = SWA_HEADS // SWA_KV_HEADS
WINDOW = 128
SB_HEADS = 4
BLOCK = 128
NUM_BUCKETS = 32
MAX_DISTANCE = 128
D_FF = 2816
ROPE_THETA = 10000.0
EPS = 1e-6

MLA_WIDTH = MLA_HEADS * MLA_V
SWA_WIDTH = SWA_HEADS * HEAD_DIM
SB_WIDTH = SB_HEADS * HEAD_DIM
D_MIX = MLA_WIDTH + SWA_WIDTH + SB_WIDTH
IN_SIZES = (Q_LORA, KV_LORA, MLA_ROPE,
            SWA_HEADS * HEAD_DIM, SWA_KV_HEADS * HEAD_DIM, SWA_KV_HEADS * HEAD_DIM,
            3 * SB_HEADS * HEAD_DIM)
D_IN = Q_LORA + KV_LORA + MLA_ROPE + (SWA_HEADS + 2 * SWA_KV_HEADS) * HEAD_DIM + 3 * SB_HEADS * HEAD_DIM

kernel_name = "hybrid_mla_swa_stickbreak_macaron"


def _split_cols(x, sizes):
    offs = np.cumsum(np.array(sizes))[:-1].tolist()
    return jnp.split(x, offs, axis=-1)


def rms_norm(x, g):
    xf = x.astype(jnp.float32)
    y = xf * lax.rsqrt(jnp.mean(xf * xf, axis=-1, keepdims=True) + EPS)
    return (y * g.astype(jnp.float32)).astype(x.dtype)


def modulate(x, g, shift, scale):
    return rms_norm(x, g) * (1.0 + scale[:, None, :]) + shift[:, None, :]


def rope(x, pos):
    half = x.shape[-1] // 2
    freqs = ROPE_THETA ** (-jnp.arange(half, dtype=jnp.float32) / half)
    ang = pos.astype(jnp.float32)[:, :, None, None] * freqs
    cos, sin = jnp.cos(ang), jnp.sin(ang)
    xf = x.astype(jnp.float32)
    x1, x2 = xf[..., :half], xf[..., half:]
    return jnp.concatenate([x1 * cos - x2 * sin, x2 * cos + x1 * sin], axis=-1).astype(x.dtype)


def t5_bucket(rel):
    n = jnp.maximum(rel, 0)
    max_exact = NUM_BUCKETS // 2
    nf = jnp.maximum(n, 1).astype(jnp.float32)
    large = max_exact + (jnp.log(nf / max_exact) / math.log(MAX_DISTANCE / max_exact)
                         * (NUM_BUCKETS - max_exact)).astype(jnp.int32)
    large = jnp.minimum(large, NUM_BUCKETS - 1)
    return jnp.where(n < max_exact, n, large)


def swiglu(h, w_gu, w_down):
    g, u = jnp.split(h @ w_gu, 2, axis=-1)
    return (jax.nn.silu(g) * u) @ w_down


def mla_attention(q, k, v):
    B, S, H, Dq = q.shape
    nb = S // BLOCK
    qb = q.reshape(B, nb, BLOCK, H, Dq).transpose(1, 0, 2, 3, 4)
    key_idx = jnp.arange(S)
    scale = Dq ** -0.5

    def one_block(args):
        i, qi = args
        s = jnp.einsum('bqhd,bkhd->bhqk', qi, k).astype(jnp.float32) * scale
        q_idx = i * BLOCK + jnp.arange(BLOCK)
        mask = key_idx[None, :] <= q_idx[:, None]
        p = jax.nn.softmax(jnp.where(mask, s, -jnp.inf), axis=-1)
        return jnp.einsum('bhqk,bkhd->bqhd', p.astype(v.dtype), v)

    out = lax.map(one_block, (jnp.arange(nb), qb))
    return out.transpose(1, 0, 2, 3, 4).reshape(B, S, H * v.shape[-1])


def swa_attention(q, k, v, pos, sinks, rel_table):
    B, S, Hq, D = q.shape
    nb = S // WINDOW

    def band(x):
        xb = x.reshape((B, nb, WINDOW) + x.shape[2:])
        prev = jnp.concatenate([jnp.zeros_like(xb[:, :1]), xb[:, :-1]], axis=1)
        return jnp.concatenate([prev, xb], axis=2)

    kb, vb = band(k), band(v)
    qb = q.reshape(B, nb, WINDOW, SWA_KV_HEADS, SWA_GROUP, D)
    s = jnp.einsum('bnqkgd,bnskd->bnkgqs', qb, kb).astype(jnp.float32) * (D ** -0.5)
    rel = pos.reshape(B, nb, WINDOW)[..., :, None] - band(pos)[..., None, :]
    bias = rel_table[t5_bucket(rel)].astype(jnp.float32)
    bias = bias.reshape(B, nb, WINDOW, 2 * WINDOW, SWA_KV_HEADS, SWA_GROUP).transpose(0, 1, 4, 5, 2, 3)
    q_in = jnp.arange(WINDOW)[:, None] + WINDOW
    k_in = jnp.arange(2 * WINDOW)[None, :]
    d = q_in - k_in
    in_win = (d >= 0) & (d < WINDOW)
    not_pad = (jnp.arange(nb)[:, None, None] > 0) | (k_in[None] >= WINDOW)
    mask = (in_win[None] & not_pad)[None, :, None, None]
    s = jnp.where(mask, s + bias, -jnp.inf)
    sink = sinks.astype(jnp.float32).reshape(SWA_KV_HEADS, SWA_GROUP)[None, None, :, :, None, None]
    m = jnp.maximum(jnp.max(s, axis=-1, keepdims=True), sink)
    p = jnp.exp(s - m)
    p = p / (jnp.sum(p, axis=-1, keepdims=True) + jnp.exp(sink - m))
    o = jnp.einsum('bnkgqs,bnskd->bnqkgd', p.astype(v.dtype), vb)
    return o.reshape(B, S, Hq * D)


def stick_breaking_attention(q, k, v):
    B, S, H, D = q.shape
    nb = S // BLOCK
    qb = q.reshape(B, nb, BLOCK, H, D).transpose(1, 0, 2, 3, 4)
    key_idx = jnp.arange(S)
    scale = D ** -0.5

    def one_block(args):
        i, qi = args
        z = jnp.einsum('bqhd,bkhd->bhqk', qi, k).astype(jnp.float32) * scale
        q_idx = i * BLOCK + jnp.arange(BLOCK)
        mask = key_idx[None, :] < q_idx[:, None]
        log_not = jnp.where(mask, jax.nn.log_sigmoid(-z), 0.0)
        after = lax.cumsum(log_not, axis=3, reverse=True) - log_not
        a = jnp.where(mask, jnp.exp(jax.nn.log_sigmoid(z) + after), 0.0)
        return jnp.einsum('bhqk,bkhd->bqhd', a.astype(v.dtype), v)

    out = lax.map(one_block, (jnp.arange(nb), qb))
    return out.transpose(1, 0, 2, 3, 4).reshape(B, S, H * D)


def setup_inputs(seed: int = 0) -> dict:
    key = jax.random.key(seed)
    ks = jax.random.split(key, 24)
    f32 = jnp.float32

    def nrm(k, shape, scale):
        return jax.random.normal(k, shape, f32) * scale

    def gain(k, shape):
        return 1.0 + 0.1 * jax.random.normal(k, shape, f32)

    x = jax.random.normal(ks[0], (BATCH, SEQ, D_MODEL), f32)
    c = jax.random.normal(ks[1], (BATCH, D_MODEL), f32)
    offset = jax.random.randint(ks[2], (BATCH, 1), 0, 4096, dtype=jnp.int32)
    positions = offset + jnp.arange(SEQ, dtype=jnp.int32)[None, :]
    return {
        "x": x,
        "c": c,
        "positions": positions,
        "rel_bias": nrm(ks[3], (NUM_BUCKETS, SWA_HEADS), 0.5),
        "norm_g": gain(ks[4], (DEPTH, 3, D_MODEL)),
        "w_mod": nrm(ks[5], (DEPTH, D_MODEL, 9 * D_MODEL), 0.5 * D_MODEL ** -0.5),
        "b_mod": nrm(ks[6], (DEPTH, 9 * D_MODEL), 0.02),
        "w_ffn1_gu": nrm(ks[7], (DEPTH, D_MODEL, 2 * D_FF), D_MODEL ** -0.5),
        "w_ffn1_down": nrm(ks[8], (DEPTH, D_FF, D_MODEL), D_FF ** -0.5),
        "w_in": nrm(ks[9], (DEPTH, D_MODEL, D_IN), D_MODEL ** -0.5),
        "q_a_norm": gain(ks[10], (DEPTH, Q_LORA)),
        "kv_a_norm": gain(ks[11], (DEPTH, KV_LORA)),
        "w_uq": nrm(ks[12], (DEPTH, Q_LORA, MLA_HEADS * MLA_QK), Q_LORA ** -0.5),
        "w_ukv": nrm(ks[13], (DEPTH, KV_LORA, MLA_HEADS * (MLA_NOPE + MLA_V)), KV_LORA ** -0.5),
        "mla_q_norm": gain(ks[14], (DEPTH, MLA_QK)),
        "mla_k_norm": gain(ks[15], (DEPTH, MLA_QK)),
        "swa_q_norm": gain(ks[16], (DEPTH, HEAD_DIM)),
        "swa_k_norm": gain(ks[17], (DEPTH, HEAD_DIM)),
        "sinks": nrm(ks[18], (DEPTH, SWA_HEADS), 0.5),
        "out_norm": gain(ks[19], (DEPTH, D_MIX)),
        "w_out": nrm(ks[20], (DEPTH, D_MIX, D_MODEL), D_MIX ** -0.5),
        "w_ffn2_gu": nrm(ks[21], (DEPTH, D_MODEL, 2 * D_FF), D_MODEL ** -0.5),
        "w_ffn2_down": nrm(ks[22], (DEPTH, D_FF, D_MODEL), D_FF ** -0.5),
    }


def reference(x, c, positions, rel_bias, norm_g, w_mod, b_mod, w_ffn1_gu, w_ffn1_down,
              w_in, q_a_norm, kv_a_norm, w_uq, w_ukv, mla_q_norm, mla_k_norm,
              swa_q_norm, swa_k_norm, sinks, out_norm, w_out, w_ffn2_gu, w_ffn2_down):
    B, S, _ = x.shape
    cond = jax.nn.silu(c)
    for l in range(DEPTH):
        mod = cond @ w_mod[l] + b_mod[l]
        sh1, sc1, g1, sh2, sc2, g2, sh3, sc3, g3 = jnp.split(mod, 9, axis=-1)

        h = modulate(x, norm_g[l, 0], sh1, sc1)
        x = x + 0.5 * g1[:, None, :] * swiglu(h, w_ffn1_gu[l], w_ffn1_down[l])

        h = modulate(x, norm_g[l, 1], sh2, sc2)
        proj = h @ w_in[l]
        c_q, c_kv, k_pe, q_s, k_s, v_s, qkv_sb = _split_cols(proj, IN_SIZES)

        qa = (rms_norm(c_q, q_a_norm[l]) @ w_uq[l]).reshape(B, S, MLA_HEADS, MLA_QK)
        qa = rms_norm(qa, mla_q_norm[l])
        qa = jnp.concatenate([qa[..., :MLA_NOPE], rope(qa[..., MLA_NOPE:], positions)], axis=-1)
        kv = (rms_norm(c_kv, kv_a_norm[l]) @ w_ukv[l]).reshape(B, S, MLA_HEADS, MLA_NOPE + MLA_V)
        k_nope, va = kv[..., :MLA_NOPE], kv[..., MLA_NOPE:]
        k_pe_h = jnp.broadcast_to(k_pe[:, :, None, :], (B, S, MLA_HEADS, MLA_ROPE))
        ka = rms_norm(jnp.concatenate([k_nope, k_pe_h], axis=-1), mla_k_norm[l])
        ka = jnp.concatenate([ka[..., :MLA_NOPE], rope(ka[..., MLA_NOPE:], positions)], axis=-1)
        o_a = mla_attention(qa, ka, va)

        qb = rms_norm(q_s.reshape(B, S, SWA_HEADS, HEAD_DIM), swa_q_norm[l])
        kb = rms_norm(k_s.reshape(B, S, SWA_KV_HEADS, HEAD_DIM), swa_k_norm[l])
        vb = v_s.reshape(B, S, SWA_KV_HEADS, HEAD_DIM)
        o_b = swa_attention(qb, kb, vb, positions, sinks[l], rel_bias)

        qc, kc, vc = jnp.split(qkv_sb.reshape(B, S, 3, SB_HEADS, HEAD_DIM), 3, axis=2)
        o_c = stick_breaking_attention(qc[:, :, 0], kc[:, :, 0], vc[:, :, 0])

        on = out_norm[l]
        mix = jnp.concatenate([
            rms_norm(o_a, on[:MLA_WIDTH]),
            rms_norm(o_b, on[MLA_WIDTH:MLA_WIDTH + SWA_WIDTH]),
            rms_norm(o_c, on[MLA_WIDTH + SWA_WIDTH:]),
        ], axis=-1)
        x = x + g2[:, None, :] * (mix @ w_out[l])

        h = modulate(x, norm_g[l, 2], sh3, sc3)
        x = x + 0.5 * g3[:, None, :] * swiglu(h, w_ffn2_gu[l], w_ffn2_down[l])
    return x
```

```python
import functools
import math

import numpy as np
import jax
import jax.numpy as jnp
from jax import lax
from jax.experimental import pallas as pl
from jax.experimental.pallas import tpu as pltpu

HEAD_DIM = 64
MLA_HEADS = 6
MLA_NOPE = 64
MLA_ROPE = 32
MLA_V = 64
MLA_QK = MLA_NOPE + MLA_ROPE
Q_LORA = 256
KV_LORA = 128
SWA_HEADS = 6
SWA_KV_HEADS = 2
SWA_GROUP = SWA_HEADS // SWA_KV_HEADS
WINDOW = 128
SB_HEADS = 4
NUM_BUCKETS = 32
MAX_DISTANCE = 128
ROPE_THETA = 10000.0
EPS = 1e-6

LANES = 128
SUBLANES = 8
VMEM_LIMIT_BYTES = 56 * 2**20

F32 = jnp.float32
BF16 = jnp.bfloat16
NEG_BIG = -1e30

PROJ_CQ = 0
PROJ_CKV = PROJ_CQ + Q_LORA
PROJ_KPE = PROJ_CKV + KV_LORA
PROJ_SWAQ = PROJ_KPE + LANES
PROJ_SWAK = PROJ_SWAQ + SWA_HEADS * HEAD_DIM
PROJ_SWAV = PROJ_SWAK + LANES
PROJ_SB = PROJ_SWAV + LANES
SB_WIDTH = SB_HEADS * HEAD_DIM
PROJ_WIDTH = PROJ_SB + 3 * SB_WIDTH
MLA_SLABS = MLA_HEADS * LANES
MLA_VW = MLA_HEADS * MLA_V
SWA_W = SWA_HEADS * HEAD_DIM
SWA_HEAD_ORDER = (0, 3, 1, 4, 2, 5)


def _cparams():
    return pltpu.CompilerParams(vmem_limit_bytes=VMEM_LIMIT_BYTES)


def _resident(shape, index_map):
    return pl.BlockSpec(shape, index_map, pipeline_mode=pl.Buffered(1))


def _sigmoid(x):
    return 1.0 / (1.0 + jnp.exp(-x))


def _modulated_norm(x, gain, shift, scale):
    ms = jnp.mean(x * x, axis=-1, keepdims=True)
    y = x * lax.rsqrt(ms + EPS) * gain
    return y * (1.0 + scale) + shift


def _dot(a, b):
    return jnp.dot(a, b, preferred_element_type=F32)


def _dot_nt(a, b):
    return lax.dot_general(a, b, (((1,), (1,)), ((), ())), preferred_element_type=F32)


def _mod_kernel(c_ref, w_ref, b_ref, o_ref):
    c = c_ref[...]
    cond = c * _sigmoid(c)
    o_ref[0] = jnp.dot(cond, w_ref[0], preferred_element_type=F32,
                       precision=lax.Precision.HIGHEST) + b_ref[0]


def _modulation(c, w_mod, b_mod):
    depth, d, n = w_mod.shape
    b = c.shape[0]
    rows = SUBLANES
    c_pad = jnp.zeros((rows, d), F32).at[:b].set(c)
    bn = n // 8
    return pl.pallas_call(
        _mod_kernel,
        grid=(depth, n // bn),
        in_specs=[
            pl.BlockSpec((rows, d), lambda l, j: (0, 0)),
            pl.BlockSpec((1, d, bn), lambda l, j: (l, 0, j)),
            pl.BlockSpec((1, 1, bn), lambda l, j: (l, 0, j)),
        ],
        out_specs=pl.BlockSpec((1, rows, bn), lambda l, j: (l, 0, j)),
        out_shape=jax.ShapeDtypeStruct((depth, rows, n), F32),
        compiler_params=_cparams(),
        name="modulation",
    )(c_pad, w_mod, b_mod.reshape(depth, 1, n))


def _rope_kernel(pos_ref, freq_ref, cos_ref, sin_ref):
    ang = pos_ref[...].astype(F32) * freq_ref[...]
    lane = lax.broadcasted_iota(jnp.int32, ang.shape, 1)
    lo = MLA_NOPE
    mid = MLA_NOPE + MLA_ROPE // 2
    hi = MLA_NOPE + MLA_ROPE
    sn = jnp.sin(ang)
    cos_ref[...] = jnp.where(lane < lo, 1.0, jnp.where(lane < hi, jnp.cos(ang), 0.0))
    sin_ref[...] = jnp.where(lane < lo, 0.0, jnp.where(lane < mid, -sn, jnp.where(lane < hi, sn, 0.0)))


def _rope_tables(positions, tm):
    t = positions.size
    half = MLA_ROPE // 2
    freqs = ROPE_THETA ** (-jnp.arange(half, dtype=F32) / half)
    row = jnp.zeros((1, LANES), F32)
    row = row.at[0, MLA_NOPE:MLA_NOPE + half].set(freqs).at[0, MLA_NOPE + half:MLA_QK].set(freqs)
    return pl.pallas_call(
        _rope_kernel,
        grid=(t // tm,),
        in_specs=[pl.BlockSpec((tm, 1), lambda i: (i, 0)), pl.BlockSpec((1, LANES), lambda i: (0, 0))],
        out_specs=[pl.BlockSpec((tm, LANES), lambda i: (i, 0))] * 2,
        out_shape=[jax.ShapeDtypeStruct((t, LANES), F32)] * 2,
        compiler_params=_cparams(),
        name="rope_tables",
    )(positions.reshape(t, 1), row)


def _rope(x, cosv, sinv, lane):
    mid = MLA_NOPE + MLA_ROPE // 2
    half = MLA_ROPE // 2
    swapped = jnp.where(lane < mid, pltpu.roll(x, LANES - half, 1), pltpu.roll(x, half, 1))
    return x * cosv + swapped * sinv


def _ffn_kernel(x_ref, sh_ref, sc_ref, gt_ref, g_ref, wgu_ref, wd_ref, o_ref, act_ref, *,
                tiles_per_batch, d_ff, chunk):
    b = pl.program_id(0) // tiles_per_batch
    x = x_ref[...]
    h = _modulated_norm(x, g_ref[0], sh_ref[0, pl.ds(b, 1), :], sc_ref[0, pl.ds(b, 1), :])
    hb = h.astype(BF16)
    for c in range(d_ff // chunk):
        g = _dot(hb, wgu_ref[:, c * chunk:(c + 1) * chunk])
        u = _dot(hb, wgu_ref[:, d_ff + c * chunk:d_ff + (c + 1) * chunk])
        act_ref[:, c * chunk:(c + 1) * chunk] = (g * _sigmoid(g) * u).astype(BF16)
    y = _dot(act_ref[...], wd_ref[...])
    o_ref[...] = x + (0.5 * gt_ref[0, pl.ds(b, 1), :]) * y


def _ffn(x, mod, gains, layer, which, w_gu, w_down, *, seq, tm):
    t, d = x.shape
    d_ff = w_down.shape[0]
    chunk = 2 * LANES
    rows = mod.shape[1]
    mod_spec = lambda k: pl.BlockSpec((1, rows, d), lambda i: (layer, 0, 3 * which + k))
    return pl.pallas_call(
        functools.partial(_ffn_kernel, tiles_per_batch=seq // tm, d_ff=d_ff, chunk=chunk),
        grid=(t // tm,),
        in_specs=[
            pl.BlockSpec((tm, d), lambda i: (i, 0)),
            mod_spec(0), mod_spec(1), mod_spec(2),
            pl.BlockSpec((1, 1, d), lambda i: (layer * 3 + which, 0, 0)),
            _resident((d, 2 * d_ff), lambda i: (0, 0)),
            _resident((d_ff, d), lambda i: (0, 0)),
        ],
        out_specs=pl.BlockSpec((tm, d), lambda i: (i, 0)),
        out_shape=jax.ShapeDtypeStruct((t, d), F32),
        scratch_shapes=[pltpu.VMEM((tm, d_ff), BF16)],
        compiler_params=_cparams(),
        name=f"ffn{which}",
    )(x, mod, mod, mod, gains, w_gu.astype(BF16), w_down.astype(BF16))


def _prologue_kernel(x_ref, sh_ref, sc_ref, g_ref, win_ref, qan_ref, kvan_ref, wuq_ref, wkn_ref, wv_ref,
                     qn_ref, kn_ref, sqn_ref, skn_ref, cos_ref, sin_ref,
                     mq_ref, mk_ref, mv_ref, sq_ref, sk_ref, sv_ref, bq_ref, bk_ref, bv_ref, *,
                     tiles_per_batch):
    b = pl.program_id(0) // tiles_per_batch
    x = x_ref[...]
    h = _modulated_norm(x, g_ref[0], sh_ref[0, pl.ds(b, 1), :], sc_ref[0, pl.ds(b, 1), :])
    proj = _dot(h.astype(BF16), win_ref[...])
    tm = x.shape[0]
    lane = lax.broadcasted_iota(jnp.int32, (tm, LANES), 1)
    low = lane < HEAD_DIM

    def rms(v, width):
        return v * lax.rsqrt(jnp.sum(v * v, axis=-1, keepdims=True) * (1.0 / width) + EPS)

    cq = rms(proj[:, PROJ_CQ:PROJ_CQ + Q_LORA], Q_LORA) * qan_ref[...]
    ckv = (rms(proj[:, PROJ_CKV:PROJ_CKV + KV_LORA], KV_LORA) * kvan_ref[...]).astype(BF16)
    q_pre = _dot(cq.astype(BF16), wuq_ref[...])
    k_pre = _dot(ckv, wkn_ref[...])
    mv_ref[...] = _dot(ckv, wv_ref[...]).astype(BF16)
    kpe = proj[:, PROJ_KPE:PROJ_KPE + LANES]
    kpe_ss = jnp.sum(kpe * kpe, axis=-1, keepdims=True)
    cosv = cos_ref[...]
    sinv = sin_ref[...]
    qn = qn_ref[...]
    kn = kn_ref[...]
    q_scale = MLA_QK ** -0.5
    for hd in range(MLA_HEADS):
        sl = slice(hd * LANES, (hd + 1) * LANES)
        qh = q_pre[:, sl]
        rq = lax.rsqrt(jnp.sum(qh * qh, axis=-1, keepdims=True) * (1.0 / MLA_QK) + EPS)
        mq_ref[:, sl] = (_rope(qh * rq * qn, cosv, sinv, lane) * q_scale).astype(BF16)
        kh = k_pre[:, sl]
        rk = lax.rsqrt((jnp.sum(kh * kh, axis=-1, keepdims=True) + kpe_ss) * (1.0 / MLA_QK) + EPS)
        mk_ref[:, sl] = _rope((kh + kpe) * rk * kn, cosv, sinv, lane).astype(BF16)

    def pair_norm(v, gain):
        sq = v * v
        lo = jnp.sum(jnp.where(low, sq, 0.0), axis=-1, keepdims=True)
        hi = jnp.sum(jnp.where(low, 0.0, sq), axis=-1, keepdims=True)
        r = jnp.where(low, lax.rsqrt(lo * (1.0 / HEAD_DIM) + EPS), lax.rsqrt(hi * (1.0 / HEAD_DIM) + EPS))
        return v * r * gain

    sqn = sqn_ref[...]
    for j in range(SWA_W // LANES):
        sl = slice(j * LANES, (j + 1) * LANES)
        v = proj[:, PROJ_SWAQ + j * LANES:PROJ_SWAQ + (j + 1) * LANES]
        sq_ref[:, sl] = (pair_norm(v, sqn) * (HEAD_DIM ** -0.5)).astype(BF16)
    sk_ref[...] = pair_norm(proj[:, PROJ_SWAK:PROJ_SWAK + LANES], skn_ref[...]).astype(BF16)
    sv_ref[...] = proj[:, PROJ_SWAV:PROJ_SWAV + LANES].astype(BF16)

    bq_ref[...] = (proj[:, PROJ_SB:PROJ_SB + SB_WIDTH] * (HEAD_DIM ** -0.5)).astype(BF16)
    bk_ref[...] = proj[:, PROJ_SB + SB_WIDTH:PROJ_SB + 2 * SB_WIDTH].astype(BF16)
    bv_ref[...] = proj[:, PROJ_SB + 2 * SB_WIDTH:PROJ_SB + 3 * SB_WIDTH].astype(BF16)


def _prologue(x, mod, gains, layer, p, cos_t, sin_t, *, seq, tm):
    t, d = x.shape
    rows = mod.shape[1]
    mod_spec = lambda k: pl.BlockSpec((1, rows, d), lambda i: (layer, 0, 3 + k))
    const = lambda shape: _resident(shape, lambda i: (0,) * len(shape))
    tok = lambda w: pl.BlockSpec((tm, w), lambda i: (i, 0))
    widths = (MLA_SLABS, MLA_SLABS, MLA_VW, SWA_W, LANES, LANES, SB_WIDTH, SB_WIDTH, SB_WIDTH)
    return pl.pallas_call(
        functools.partial(_prologue_kernel, tiles_per_batch=seq // tm),
        grid=(t // tm,),
        in_specs=[
            tok(d), mod_spec(0), mod_spec(1),
            pl.BlockSpec((1, 1, d), lambda i: (layer * 3 + 1, 0, 0)),
            const((d, PROJ_WIDTH)), const((1, Q_LORA)), const((1, KV_LORA)),
            const((Q_LORA, MLA_SLABS)), const((KV_LORA, MLA_SLABS)), const((KV_LORA, MLA_VW)),
            const((1, LANES)), const((1, LANES)), const((1, LANES)), const((1, LANES)),
            tok(LANES), tok(LANES),
        ],
        out_specs=[tok(w) for w in widths],
        out_shape=[jax.ShapeDtypeStruct((t, w), BF16) for w in widths],
        compiler_params=_cparams(),
        name="attn_prologue",
    )(x, mod, mod, gains, p["w_in"], p["q_a_norm"], p["kv_a_norm"], p["w_uq"], p["w_kn"], p["w_v"],
      p["mla_q_norm"], p["mla_k_norm"], p["swa_q_norm"], p["swa_k_norm"], cos_t, sin_t)


def _mla_kernel(q_ref, k_ref, v_ref, o_ref, *, blk):
    i = pl.program_id(2)
    lane = lax.broadcasted_iota(jnp.int32, (blk, LANES), 1)
    row = lax.broadcasted_iota(jnp.int32, (blk, blk), 0)
    col = lax.broadcasted_iota(jnp.int32, (blk, blk), 1)
    causal = col <= row
    qs = (q_ref[:, 0:LANES], q_ref[:, LANES:2 * LANES])

    def block(j, carry, masked):
        ks = pl.multiple_of(j * blk, blk)
        kj = k_ref[pl.ds(ks, blk), :]
        vj = v_ref[pl.ds(ks, blk), :]
        out = []
        for hd in range(2):
            m, l, acc = carry[hd]
            s = _dot_nt(qs[hd], kj[:, hd * LANES:(hd + 1) * LANES])
            if masked:
                s = jnp.where(causal, s, NEG_BIG)
            m_new = jnp.maximum(m, jnp.max(s, axis=-1, keepdims=True))
            alpha = jnp.exp(m - m_new)
            p = jnp.exp(s - m_new)
            l = alpha * l + jnp.sum(p, axis=-1, keepdims=True)
            acc = alpha * acc + _dot(p.astype(BF16), vj)
            out.append((m_new, l, acc))
        return tuple(out)

    init = tuple((jnp.full((blk, 1), NEG_BIG, F32), jnp.zeros((blk, 1), F32), jnp.zeros((blk, LANES), F32))
                 for _ in range(2))
    carry = lax.fori_loop(0, i, lambda j, c: block(j, c, False), init)
    (_, l0, a0), (_, l1, a1) = block(i, carry, True)
    o_ref[...] = jnp.where(lane < MLA_V, a0 * (1.0 / l0), a1 * (1.0 / l1)).astype(BF16)


def _mla_attention(q, k, v, *, batch, seq, blk):
    t = q.shape[0]
    nq = seq // blk
    pairs = MLA_HEADS // 2
    return pl.pallas_call(
        functools.partial(_mla_kernel, blk=blk),
        grid=(batch, pairs, nq),
        in_specs=[
            pl.BlockSpec((blk, 2 * LANES), lambda b, p, i: (b * nq + i, p)),
            pl.BlockSpec((seq, 2 * LANES), lambda b, p, i: (b, p)),
            pl.BlockSpec((seq, LANES), lambda b, p, i: (b, p)),
        ],
        out_specs=pl.BlockSpec((blk, LANES), lambda b, p, i: (b * nq + i, p)),
        out_shape=jax.ShapeDtypeStruct((t, MLA_VW), BF16),
        compiler_params=_cparams(),
        name="mla_attention",
    )(q, k, v)


def _t5_bucket(rel):
    n = jnp.maximum(rel, 0)
    max_exact = NUM_BUCKETS // 2
    nf = jnp.maximum(n, 1).astype(F32)
    large = max_exact + (jnp.log(nf / max_exact) / math.log(MAX_DISTANCE / max_exact)
                         * (NUM_BUCKETS - max_exact)).astype(jnp.int32)
    large = jnp.minimum(large, NUM_BUCKETS - 1)
    return jnp.where(n < max_exact, n, large)


def _swa_kernel(q_ref, k_ref, v_ref, posq_ref, posk_ref, tab_ref, sink_ref, o_ref, *, tq):
    i = pl.program_id(1)
    w = WINDOW
    dist = lax.broadcasted_iota(jnp.int32, (SUBLANES, LANES), 1)
    bucket = _t5_bucket(dist)
    luts = []
    for hd in range(SWA_HEADS):
        lut = jnp.zeros((SUBLANES, LANES), F32)
        for kk in range(NUM_BUCKETS):
            lut = jnp.where(bucket == kk, tab_ref[kk, hd], lut)
        luts.append(jnp.broadcast_to(lut[0:1, :], (w, LANES)))
    lane = lax.broadcasted_iota(jnp.int32, (w, LANES), 1)
    low = lane < HEAD_DIM
    qrow = lax.broadcasted_iota(jnp.int32, (w, 2 * w), 0)
    kcol = lax.broadcasted_iota(jnp.int32, (w, 2 * w), 1)
    for blk in range(tq // w):
        q0 = (i * (tq // w) + blk) * w
        ks = pl.multiple_of(jnp.maximum(q0 - w, 0), w)
        kb = k_ref[pl.ds(ks, 2 * w), :]
        vb = v_ref[pl.ds(ks, 2 * w), :]
        rel = posq_ref[blk * w:(blk + 1) * w, :] - posk_ref[0, :, pl.ds(ks, 2 * w)]
        idx = jnp.clip(rel, 0, LANES - 1)
        d = (q0 + qrow) - (ks + kcol)
        valid = (d >= 0) & (d < w)
        for j in range(SWA_W // LANES):
            qs = q_ref[blk * w:(blk + 1) * w, j * LANES:(j + 1) * LANES]
            halves = []
            for half in range(2):
                hd = SWA_HEAD_ORDER[2 * j + half]
                qm = jnp.where(low, qs, jnp.zeros_like(qs)) if half == 0 else jnp.where(low, jnp.zeros_like(qs), qs)
                s = _dot_nt(qm, kb)
                bias = jnp.concatenate(
                    [jnp.take_along_axis(luts[hd], idx[:, :LANES], axis=1),
                     jnp.take_along_axis(luts[hd], idx[:, LANES:], axis=1)], axis=1)
                s = jnp.where(valid, s + bias, NEG_BIG)
                sink = sink_ref[hd]
                m = jnp.maximum(jnp.max(s, axis=-1, keepdims=True), sink)
                p = jnp.exp(s - m)
                den = jnp.sum(p, axis=-1, keepdims=True) + jnp.exp(sink - m)
                halves.append(_dot((p * (1.0 / den)).astype(BF16), vb))
            o_ref[blk * w:(blk + 1) * w, j * LANES:(j + 1) * LANES] = jnp.where(low, halves[0], halves[1]).astype(BF16)


def _swa_attention(q, k, v, positions, rel_bias, sinks, *, batch, seq, tq):
    t = q.shape[0]
    nq = seq // tq
    return pl.pallas_call(
        functools.partial(_swa_kernel, tq=tq),
        grid=(batch, nq),
        in_specs=[
            pl.BlockSpec((tq, SWA_W), lambda b, i: (b * nq + i, 0)),
            pl.BlockSpec((seq, LANES), lambda b, i: (b, 0)),
            pl.BlockSpec((seq, LANES), lambda b, i: (b, 0)),
            pl.BlockSpec((tq, 1), lambda b, i: (b * nq + i, 0)),
            pl.BlockSpec((1, 1, seq), lambda b, i: (b, 0, 0)),
            pl.BlockSpec(memory_space=pltpu.SMEM),
            pl.BlockSpec(memory_space=pltpu.SMEM),
        ],
        out_specs=pl.BlockSpec((tq, SWA_W), lambda b, i: (b * nq + i, 0)),
        out_shape=jax.ShapeDtypeStruct((t, SWA_W), BF16),
        compiler_params=_cparams(),
        name="swa_attention",
    )(q, k, v, positions.reshape(t, 1), positions.reshape(batch, 1, seq), rel_bias, sinks)


def _sb_kernel(q_ref, k_ref, v_ref, u_ref, o_ref, *, blk):
    i = pl.program_id(2)
    lane = lax.broadcasted_iota(jnp.int32, (blk, LANES), 1)
    low = lane < HEAD_DIM
    row = lax.broadcasted_iota(jnp.int32, (blk, blk), 0)
    col = lax.broadcasted_iota(jnp.int32, (blk, blk), 1)
    strict = col < row
    q = q_ref[...]
    zero = jnp.zeros_like(q)
    qs = (jnp.where(low, q, zero), jnp.where(low, zero, q))
    tri = u_ref[...]

    def block(j, carry, masked):
        ks = pl.multiple_of(j * blk, blk)
        kj = k_ref[pl.ds(ks, blk), :]
        vj = v_ref[pl.ds(ks, blk), :]
        out = []
        for hd in range(2):
            c, acc = carry[hd]
            z = _dot_nt(qs[hd], kj)
            log_not = -(jnp.maximum(z, 0.0) + jnp.log(1.0 + jnp.exp(-jnp.abs(z))))
            if masked:
                log_not = jnp.where(strict, log_not, 0.0)
            hi = log_not.astype(BF16)
            lo = (log_not - hi.astype(F32)).astype(BF16)
            after = _dot(hi, tri) + _dot(lo, tri) + c
            a = jnp.exp(log_not + z + after)
            if masked:
                a = jnp.where(strict, a, 0.0)
            acc = acc + _dot(a.astype(BF16), vj)
            c = c + jnp.sum(log_not, axis=-1, keepdims=True)
            out.append((c, acc))
        return tuple(out)

    init = tuple((jnp.zeros((blk, 1), F32), jnp.zeros((blk, LANES), F32)) for _ in range(2))
    carry = block(i, init, True)
    (_, a0), (_, a1) = lax.fori_loop(0, i, lambda jj, c: block(i - 1 - jj, c, False), carry)
    o_ref[...] = jnp.where(low, a0, a1).astype(BF16)


def _sb_attention(q, k, v, *, batch, seq, blk):
    t = q.shape[0]
    nq = seq // blk
    pairs = SB_HEADS // 2
    tri = jnp.tril(jnp.ones((blk, blk), BF16), k=-1)
    return pl.pallas_call(
        functools.partial(_sb_kernel, blk=blk),
        grid=(batch, pairs, nq),
        in_specs=[
            pl.BlockSpec((blk, LANES), lambda b, p, i: (b * nq + i, p)),
            pl.BlockSpec((seq, LANES), lambda b, p, i: (b, p)),
            pl.BlockSpec((seq, LANES), lambda b, p, i: (b, p)),
            _resident((blk, blk), lambda b, p, i: (0, 0)),
        ],
        out_specs=pl.BlockSpec((blk, LANES), lambda b, p, i: (b * nq + i, p)),
        out_shape=jax.ShapeDtypeStruct((t, SB_WIDTH), BF16),
        compiler_params=_cparams(),
        name="sb_attention",
    )(q, k, v, tri)


def _mix_kernel(x_ref, oa_ref, ob_ref, oc_ref, gt_ref, na_ref, nb_ref, nc_ref, wa_ref, wb_ref, wc_ref, o_ref, *,
                tiles_per_batch):
    b = pl.program_id(0) // tiles_per_batch

    def group(o_ref_, n_ref_):
        o = o_ref_[...].astype(F32)
        return (o * lax.rsqrt(jnp.mean(o * o, axis=-1, keepdims=True) + EPS) * n_ref_[...]).astype(BF16)

    y = (_dot(group(oa_ref, na_ref), wa_ref[...]) + _dot(group(ob_ref, nb_ref), wb_ref[...])
         + _dot(group(oc_ref, nc_ref), wc_ref[...]))
    o_ref[...] = x_ref[...] + gt_ref[0, pl.ds(b, 1), :] * y


def _mix(x, oa, ob, oc, mod, layer, p, *, seq, tm):
    t, d = x.shape
    rows = mod.shape[1]
    const = lambda shape: _resident(shape, lambda i: (0,) * len(shape))
    tok = lambda w: pl.BlockSpec((tm, w), lambda i: (i, 0))
    return pl.pallas_call(
        functools.partial(_mix_kernel, tiles_per_batch=seq // tm),
        grid=(t // tm,),
        in_specs=[
            tok(d), tok(MLA_VW), tok(SWA_W), tok(SB_WIDTH),
            pl.BlockSpec((1, rows, d), lambda i: (layer, 0, 5)),
            const((1, MLA_VW)), const((1, SWA_W)), const((1, SB_WIDTH)),
            const((MLA_VW, d)), const((SWA_W, d)), const((SB_WIDTH, d)),
        ],
        out_specs=tok(d),
        out_shape=jax.ShapeDtypeStruct((t, d), F32),
        compiler_params=_cparams(),
        name="mix_out_proj",
    )(x, oa, ob, oc, mod, p["on_a"], p["on_b"], p["on_c"], p["w_out_a"], p["w_out_b"], p["w_out_c"])


def _proj_columns():
    idx = np.full((PROJ_WIDTH,), -1, np.int64)
    o_ckv = Q_LORA
    o_kpe = o_ckv + KV_LORA
    o_qs = o_kpe + MLA_ROPE
    o_ks = o_qs + SWA_HEADS * HEAD_DIM
    o_vs = o_ks + SWA_KV_HEADS * HEAD_DIM
    o_sb = o_vs + SWA_KV_HEADS * HEAD_DIM
    idx[PROJ_CQ:PROJ_CQ + Q_LORA] = np.arange(Q_LORA)
    idx[PROJ_CKV:PROJ_CKV + KV_LORA] = o_ckv + np.arange(KV_LORA)
    idx[PROJ_KPE + MLA_NOPE:PROJ_KPE + MLA_QK] = o_kpe + np.arange(MLA_ROPE)
    for slot, hd in enumerate(SWA_HEAD_ORDER):
        idx[PROJ_SWAQ + slot * HEAD_DIM:PROJ_SWAQ + (slot + 1) * HEAD_DIM] = o_qs + hd * HEAD_DIM + np.arange(HEAD_DIM)
    idx[PROJ_SWAK:PROJ_SWAK + LANES] = o_ks + np.arange(LANES)
    idx[PROJ_SWAV:PROJ_SWAV + LANES] = o_vs + np.arange(LANES)
    idx[PROJ_SB:PROJ_SB + 3 * SB_WIDTH] = o_sb + np.arange(3 * SB_WIDTH)
    return idx


def _take_columns(w, idx):
    w_ext = jnp.concatenate([w, jnp.zeros((w.shape[0], 1), w.dtype)], axis=1)
    return w_ext[:, np.where(idx < 0, w.shape[1], idx)]


def _layer_params(l, w_in, q_a_norm, kv_a_norm, w_uq, w_ukv, mla_q_norm, mla_k_norm, swa_q_norm, swa_k_norm,
                  sinks, out_norm, w_out):
    uq_idx = np.full((MLA_SLABS,), -1, np.int64)
    kn_idx = np.full((MLA_SLABS,), -1, np.int64)
    v_idx = np.zeros((MLA_VW,), np.int64)
    for hd in range(MLA_HEADS):
        uq_idx[hd * LANES:hd * LANES + MLA_QK] = hd * MLA_QK + np.arange(MLA_QK)
        kn_idx[hd * LANES:hd * LANES + MLA_NOPE] = hd * (MLA_NOPE + MLA_V) + np.arange(MLA_NOPE)
        v_idx[hd * MLA_V:(hd + 1) * MLA_V] = hd * (MLA_NOPE + MLA_V) + MLA_NOPE + np.arange(MLA_V)
    pad = lambda g: jnp.zeros((1, LANES), F32).at[0, :MLA_QK].set(g)
    pair = lambda g: jnp.concatenate([g, g]).reshape(1, LANES)
    swa_rows = np.concatenate([MLA_VW + hd * HEAD_DIM + np.arange(HEAD_DIM) for hd in SWA_HEAD_ORDER])
    return {
        "w_in": _take_columns(w_in[l], _proj_columns()).astype(BF16),
        "q_a_norm": q_a_norm[l].reshape(1, Q_LORA),
        "kv_a_norm": kv_a_norm[l].reshape(1, KV_LORA),
        "w_uq": _take_columns(w_uq[l], uq_idx).astype(BF16),
        "w_kn": _take_columns(w_ukv[l], kn_idx).astype(BF16),
        "w_v": w_ukv[l][:, v_idx].astype(BF16),
        "mla_q_norm": pad(mla_q_norm[l]),
        "mla_k_norm": pad(mla_k_norm[l]),
        "swa_q_norm": pair(swa_q_norm[l]),
        "swa_k_norm": pair(swa_k_norm[l]),
        "sinks": sinks[l],
        "on_a": out_norm[l][:MLA_VW].reshape(1, MLA_VW),
        "on_b": out_norm[l][swa_rows].reshape(1, SWA_W),
        "on_c": out_norm[l][MLA_VW + SWA_W:].reshape(1, SB_WIDTH),
        "w_out_a": w_out[l][:MLA_VW].astype(BF16),
        "w_out_b": w_out[l][swa_rows].astype(BF16),
        "w_out_c": w_out[l][MLA_VW + SWA_W:].astype(BF16),
    }


def kernel(x, c, positions, rel_bias, norm_g, w_mod, b_mod, w_ffn1_gu, w_ffn1_down, w_in, q_a_norm, kv_a_norm,
           w_uq, w_ukv, mla_q_norm, mla_k_norm, swa_q_norm, swa_k_norm, sinks, out_norm, w_out, w_ffn2_gu,
           w_ffn2_down):
    batch, seq, d = x.shape
    depth = w_mod.shape[0]
    t = batch * seq
    tm = min(512, seq)
    mla_blk = min(512, seq)
    sb_blk = min(256, seq)
    swa_tq = min(512, seq)

    xt = x.reshape(t, d)
    mod = _modulation(c, w_mod, b_mod)
    gains = norm_g.reshape(depth * 3, 1, d)
    cos_t, sin_t = _rope_tables(positions, tm)
    for l in range(depth):
        p = _layer_params(l, w_in, q_a_norm, kv_a_norm, w_uq, w_ukv, mla_q_norm, mla_k_norm, swa_q_norm,
                          swa_k_norm, sinks, out_norm, w_out)
        xt = _ffn(xt, mod, gains, l, 0, w_ffn1_gu[l], w_ffn1_down[l], seq=seq, tm=tm)
        mq, mk, mv, sq, sk, sv, bq, bk, bv = _prologue(xt, mod, gains, l, p, cos_t, sin_t, seq=seq, tm=tm)
        oa = _mla_attention(mq, mk, mv, batch=batch, seq=seq, blk=mla_blk)
        ob = _swa_attention(sq, sk, sv, positions, rel_bias, p["sinks"], batch=batch, seq=seq, tq=swa_tq)
        oc = _sb_attention(bq, bk, bv, batch=batch, seq=seq, blk=sb_blk)
        xt = _mix(xt, oa, ob, oc, mod, l, p, seq=seq, tm=tm)
        xt = _ffn(xt, mod, gains, l, 2, w_ffn2_gu[l], w_ffn2_down[l], seq=seq, tm=tm)
    return xt.reshape(batch, seq, d)
```

```python
import functools
import math

import numpy as np
import jax
import jax.numpy as jnp
from jax import lax
from jax.experimental import pallas as pl
from jax.experimental.pallas import tpu as pltpu

HEAD_DIM = 64
MLA_HEADS = 6
MLA_NOPE = 64
MLA_ROPE = 32
MLA_V = 64
MLA_QK = MLA_NOPE + MLA_ROPE
Q_LORA = 256
KV_LORA = 128
SWA_HEADS = 6
SWA_KV_HEADS = 2
SWA_GROUP = SWA_HEADS // SWA_KV_HEADS
WINDOW = 128
SB_HEADS = 4
NUM_BUCKETS = 32
MAX_DISTANCE = 128
ROPE_THETA = 10000.0
EPS = 1e-6

LANES = 128
SUBLANES = 8
VMEM_LIMIT_BYTES = 56 * 2**20

F32 = jnp.float32
BF16 = jnp.bfloat16
NEG_BIG = -1e30
LOG2E = math.log2(math.e)

PROJ_CQ = 0
PROJ_CKV = PROJ_CQ + Q_LORA
PROJ_KPE = PROJ_CKV + KV_LORA
PROJ_SWAQ = PROJ_KPE + LANES
PROJ_SWAK = PROJ_SWAQ + SWA_HEADS * HEAD_DIM
PROJ_SWAV = PROJ_SWAK + LANES
PROJ_SB = PROJ_SWAV + LANES
SB_WIDTH = SB_HEADS * HEAD_DIM
PROJ_WIDTH = PROJ_SB + 3 * SB_WIDTH
MLA_SLABS = MLA_HEADS * LANES
MLA_VW = MLA_HEADS * MLA_V
SWA_W = SWA_HEADS * HEAD_DIM
SWA_HEAD_ORDER = (0, 3, 1, 4, 2, 5)


def _cparams():
    return pltpu.CompilerParams(vmem_limit_bytes=VMEM_LIMIT_BYTES)


def _resident(shape, index_map):
    return pl.BlockSpec(shape, index_map, pipeline_mode=pl.Buffered(1))


def _sigmoid(x):
    return 1.0 / (1.0 + jnp.exp(-x))


def _modulated_norm(x, gain, shift, scale):
    ms = jnp.mean(x * x, axis=-1, keepdims=True)
    y = x * lax.rsqrt(ms + EPS) * gain
    return y * (1.0 + scale) + shift


def _dot(a, b):
    return jnp.dot(a, b, preferred_element_type=F32)


def _dot_nt(a, b):
    return lax.dot_general(a, b, (((1,), (1,)), ((), ())), preferred_element_type=F32)


def _mod_kernel(c_ref, w_ref, b_ref, o_ref):
    c = c_ref[...]
    cond = c * _sigmoid(c)
    o_ref[0] = jnp.dot(cond, w_ref[0], preferred_element_type=F32,
                       precision=lax.Precision.HIGHEST) + b_ref[0]


def _modulation(c, w_mod, b_mod):
    depth, d, n = w_mod.shape
    b = c.shape[0]
    rows = SUBLANES
    c_pad = jnp.zeros((rows, d), F32).at[:b].set(c)
    bn = n // 8
    return pl.pallas_call(
        _mod_kernel,
        grid=(depth, n // bn),
        in_specs=[
            pl.BlockSpec((rows, d), lambda l, j: (0, 0)),
            pl.BlockSpec((1, d, bn), lambda l, j: (l, 0, j)),
            pl.BlockSpec((1, 1, bn), lambda l, j: (l, 0, j)),
        ],
        out_specs=pl.BlockSpec((1, rows, bn), lambda l, j: (l, 0, j)),
        out_shape=jax.ShapeDtypeStruct((depth, rows, n), F32),
        compiler_params=_cparams(),
        name="modulation",
    )(c_pad, w_mod, b_mod.reshape(depth, 1, n))


def _rope_kernel(pos_ref, freq_ref, cos_ref, sin_ref):
    ang = pos_ref[...].astype(F32) * freq_ref[...]
    lane = lax.broadcasted_iota(jnp.int32, ang.shape, 1)
    lo = MLA_NOPE
    mid = MLA_NOPE + MLA_ROPE // 2
    hi = MLA_NOPE + MLA_ROPE
    sn = jnp.sin(ang)
    cos_ref[...] = jnp.where(lane < lo, 1.0, jnp.where(lane < hi, jnp.cos(ang), 0.0))
    sin_ref[...] = jnp.where(lane < lo, 0.0, jnp.where(lane < mid, -sn, jnp.where(lane < hi, sn, 0.0)))


def _rope_tables(positions, tm):
    t = positions.size
    half = MLA_ROPE // 2
    freqs = ROPE_THETA ** (-jnp.arange(half, dtype=F32) / half)
    row = jnp.zeros((1, LANES), F32)
    row = row.at[0, MLA_NOPE:MLA_NOPE + half].set(freqs).at[0, MLA_NOPE + half:MLA_QK].set(freqs)
    return pl.pallas_call(
        _rope_kernel,
        grid=(t // tm,),
        in_specs=[pl.BlockSpec((tm, 1), lambda i: (i, 0)), pl.BlockSpec((1, LANES), lambda i: (0, 0))],
        out_specs=[pl.BlockSpec((tm, LANES), lambda i: (i, 0))] * 2,
        out_shape=[jax.ShapeDtypeStruct((t, LANES), F32)] * 2,
        compiler_params=_cparams(),
        name="rope_tables",
    )(positions.reshape(t, 1), row)


def _rope(x, cosv, sinv, lane):
    mid = MLA_NOPE + MLA_ROPE // 2
    half = MLA_ROPE // 2
    swapped = jnp.where(lane < mid, pltpu.roll(x, LANES - half, 1), pltpu.roll(x, half, 1))
    return x * cosv + swapped * sinv


def _ffn_kernel(x_ref, sh_ref, sc_ref, gt_ref, g_ref, wgu_ref, wd_ref, o_ref, act_ref, *,
                tiles_per_batch, d_ff, chunk):
    b = pl.program_id(0) // tiles_per_batch
    x = x_ref[...]
    h = _modulated_norm(x, g_ref[0], sh_ref[0, pl.ds(b, 1), :], sc_ref[0, pl.ds(b, 1), :])
    hb = h.astype(BF16)
    for c in range(d_ff // chunk):
        g = _dot(hb, wgu_ref[:, c * chunk:(c + 1) * chunk])
        u = _dot(hb, wgu_ref[:, d_ff + c * chunk:d_ff + (c + 1) * chunk])
        act_ref[:, c * chunk:(c + 1) * chunk] = (g * _sigmoid(g) * u).astype(BF16)
    y = _dot(act_ref[...], wd_ref[...])
    o_ref[...] = x + (0.5 * gt_ref[0, pl.ds(b, 1), :]) * y


def _ffn(x, mod, gains, layer, which, w_gu, w_down, *, seq, tm):
    t, d = x.shape
    d_ff = w_down.shape[0]
    chunk = 2 * LANES
    rows = mod.shape[1]
    mod_spec = lambda k: pl.BlockSpec((1, rows, d), lambda i: (layer, 0, 3 * which + k))
    return pl.pallas_call(
        functools.partial(_ffn_kernel, tiles_per_batch=seq // tm, d_ff=d_ff, chunk=chunk),
        grid=(t // tm,),
        in_specs=[
            pl.BlockSpec((tm, d), lambda i: (i, 0)),
            mod_spec(0), mod_spec(1), mod_spec(2),
            pl.BlockSpec((1, 1, d), lambda i: (layer * 3 + which, 0, 0)),
            _resident((d, 2 * d_ff), lambda i: (0, 0)),
            _resident((d_ff, d), lambda i: (0, 0)),
        ],
        out_specs=pl.BlockSpec((tm, d), lambda i: (i, 0)),
        out_shape=jax.ShapeDtypeStruct((t, d), F32),
        scratch_shapes=[pltpu.VMEM((tm, d_ff), BF16)],
        compiler_params=_cparams(),
        name=f"ffn{which}",
    )(x, mod, mod, mod, gains, w_gu.astype(BF16), w_down.astype(BF16))


def _prologue_kernel(x_ref, sh_ref, sc_ref, g_ref, win_ref, qan_ref, kvan_ref, wuq_ref, wkn_ref, wv_ref,
                     qn_ref, kn_ref, sqn_ref, skn_ref, cos_ref, sin_ref,
                     mq_ref, mk_ref, mv_ref, sq_ref, sk_ref, sv_ref, bq_ref, bk_ref, bv_ref, *,
                     tiles_per_batch):
    b = pl.program_id(0) // tiles_per_batch
    x = x_ref[...]
    h = _modulated_norm(x, g_ref[0], sh_ref[0, pl.ds(b, 1), :], sc_ref[0, pl.ds(b, 1), :])
    proj = _dot(h.astype(BF16), win_ref[...])
    tm = x.shape[0]
    lane = lax.broadcasted_iota(jnp.int32, (tm, LANES), 1)
    low = lane < HEAD_DIM

    def rms(v, width):
        return v * lax.rsqrt(jnp.sum(v * v, axis=-1, keepdims=True) * (1.0 / width) + EPS)

    cq = rms(proj[:, PROJ_CQ:PROJ_CQ + Q_LORA], Q_LORA) * qan_ref[...]
    ckv = (rms(proj[:, PROJ_CKV:PROJ_CKV + KV_LORA], KV_LORA) * kvan_ref[...]).astype(BF16)
    q_pre = _dot(cq.astype(BF16), wuq_ref[...])
    k_pre = _dot(ckv, wkn_ref[...])
    mv_ref[...] = _dot(ckv, wv_ref[...]).astype(BF16)
    kpe = proj[:, PROJ_KPE:PROJ_KPE + LANES]
    kpe_ss = jnp.sum(kpe * kpe, axis=-1, keepdims=True)
    cosv = cos_ref[...]
    sinv = sin_ref[...]
    qn = qn_ref[...]
    kn = kn_ref[...]
    q_scale = MLA_QK ** -0.5
    for hd in range(MLA_HEADS):
        sl = slice(hd * LANES, (hd + 1) * LANES)
        qh = q_pre[:, sl]
        rq = lax.rsqrt(jnp.sum(qh * qh, axis=-1, keepdims=True) * (1.0 / MLA_QK) + EPS)
        mq_ref[:, sl] = (_rope(qh * rq * qn, cosv, sinv, lane) * q_scale).astype(BF16)
        kh = k_pre[:, sl]
        rk = lax.rsqrt((jnp.sum(kh * kh, axis=-1, keepdims=True) + kpe_ss) * (1.0 / MLA_QK) + EPS)
        mk_ref[:, sl] = _rope((kh + kpe) * rk * kn, cosv, sinv, lane).astype(BF16)

    def pair_norm(v, gain):
        sq = v * v
        lo = jnp.sum(jnp.where(low, sq, 0.0), axis=-1, keepdims=True)
        hi = jnp.sum(jnp.where(low, 0.0, sq), axis=-1, keepdims=True)
        r = jnp.where(low, lax.rsqrt(lo * (1.0 / HEAD_DIM) + EPS), lax.rsqrt(hi * (1.0 / HEAD_DIM) + EPS))
        return v * r * gain

    sqn = sqn_ref[...]
    for j in range(SWA_W // LANES):
        sl = slice(j * LANES, (j + 1) * LANES)
        v = proj[:, PROJ_SWAQ + j * LANES:PROJ_SWAQ + (j + 1) * LANES]
        sq_ref[:, sl] = (pair_norm(v, sqn) * (HEAD_DIM ** -0.5)).astype(BF16)
    sk_ref[...] = pair_norm(proj[:, PROJ_SWAK:PROJ_SWAK + LANES], skn_ref[...]).astype(BF16)
    sv_ref[...] = proj[:, PROJ_SWAV:PROJ_SWAV + LANES].astype(BF16)

    bq_ref[...] = (proj[:, PROJ_SB:PROJ_SB + SB_WIDTH] * (HEAD_DIM ** -0.5)).astype(BF16)
    bk_ref[...] = proj[:, PROJ_SB + SB_WIDTH:PROJ_SB + 2 * SB_WIDTH].astype(BF16)
    bv_ref[...] = proj[:, PROJ_SB + 2 * SB_WIDTH:PROJ_SB + 3 * SB_WIDTH].astype(BF16)


def _prologue(x, mod, gains, layer, p, cos_t, sin_t, *, seq, tm):
    t, d = x.shape
    rows = mod.shape[1]
    mod_spec = lambda k: pl.BlockSpec((1, rows, d), lambda i: (layer, 0, 3 + k))
    const = lambda shape: _resident(shape, lambda i: (0,) * len(shape))
    tok = lambda w: pl.BlockSpec((tm, w), lambda i: (i, 0))
    widths = (MLA_SLABS, MLA_SLABS, MLA_VW, SWA_W, LANES, LANES, SB_WIDTH, SB_WIDTH, SB_WIDTH)
    return pl.pallas_call(
        functools.partial(_prologue_kernel, tiles_per_batch=seq // tm),
        grid=(t // tm,),
        in_specs=[
            tok(d), mod_spec(0), mod_spec(1),
            pl.BlockSpec((1, 1, d), lambda i: (layer * 3 + 1, 0, 0)),
            const((d, PROJ_WIDTH)), const((1, Q_LORA)), const((1, KV_LORA)),
            const((Q_LORA, MLA_SLABS)), const((KV_LORA, MLA_SLABS)), const((KV_LORA, MLA_VW)),
            const((1, LANES)), const((1, LANES)), const((1, LANES)), const((1, LANES)),
            tok(LANES), tok(LANES),
        ],
        out_specs=[tok(w) for w in widths],
        out_shape=[jax.ShapeDtypeStruct((t, w), BF16) for w in widths],
        compiler_params=_cparams(),
        name="attn_prologue",
    )(x, mod, mod, gains, p["w_in"], p["q_a_norm"], p["kv_a_norm"], p["w_uq"], p["w_kn"], p["w_v"],
      p["mla_q_norm"], p["mla_k_norm"], p["swa_q_norm"], p["swa_k_norm"], cos_t, sin_t)


def _mla_kernel(q_ref, k_ref, v_ref, o_ref, *, blk):
    i = pl.program_id(2)
    lane = lax.broadcasted_iota(jnp.int32, (blk, LANES), 1)
    row = lax.broadcasted_iota(jnp.int32, (blk, blk), 0)
    col = lax.broadcasted_iota(jnp.int32, (blk, blk), 1)
    causal = col <= row
    qs = (q_ref[:, 0:LANES], q_ref[:, LANES:2 * LANES])

    def block(j, carry, masked):
        ks = pl.multiple_of(j * blk, blk)
        kj = k_ref[pl.ds(ks, blk), :]
        vj = v_ref[pl.ds(ks, blk), :]
        out = []
        for hd in range(2):
            m, l, acc = carry[hd]
            s = _dot_nt(qs[hd], kj[:, hd * LANES:(hd + 1) * LANES])
            if masked:
                s = jnp.where(causal, s, NEG_BIG)
            m_new = jnp.maximum(m, jnp.max(s, axis=-1, keepdims=True))
            alpha = jnp.exp(m - m_new)
            p = jnp.exp(s - m_new)
            l = alpha * l + jnp.sum(p, axis=-1, keepdims=True)
            acc = alpha * acc + _dot(p.astype(BF16), vj)
            out.append((m_new, l, acc))
        return tuple(out)

    init = tuple((jnp.full((blk, 1), NEG_BIG, F32), jnp.zeros((blk, 1), F32), jnp.zeros((blk, LANES), F32))
                 for _ in range(2))
    carry = lax.fori_loop(0, i, lambda j, c: block(j, c, False), init)
    (_, l0, a0), (_, l1, a1) = block(i, carry, True)
    o_ref[...] = jnp.where(lane < MLA_V, a0 * (1.0 / l0), a1 * (1.0 / l1)).astype(BF16)


def _mla_attention(q, k, v, *, batch, seq, blk):
    t = q.shape[0]
    nq = seq // blk
    pairs = MLA_HEADS // 2
    return pl.pallas_call(
        functools.partial(_mla_kernel, blk=blk),
        grid=(batch, pairs, nq),
        in_specs=[
            pl.BlockSpec((blk, 2 * LANES), lambda b, p, i: (b * nq + i, p)),
            pl.BlockSpec((seq, 2 * LANES), lambda b, p, i: (b, p)),
            pl.BlockSpec((seq, LANES), lambda b, p, i: (b, p)),
        ],
        out_specs=pl.BlockSpec((blk, LANES), lambda b, p, i: (b * nq + i, p)),
        out_shape=jax.ShapeDtypeStruct((t, MLA_VW), BF16),
        compiler_params=_cparams(),
        name="mla_attention",
    )(q, k, v)


def _t5_bucket(rel):
    n = jnp.maximum(rel, 0)
    max_exact = NUM_BUCKETS // 2
    nf = jnp.maximum(n, 1).astype(F32)
    large = max_exact + (jnp.log(nf / max_exact) / math.log(MAX_DISTANCE / max_exact)
                         * (NUM_BUCKETS - max_exact)).astype(jnp.int32)
    large = jnp.minimum(large, NUM_BUCKETS - 1)
    return jnp.where(n < max_exact, n, large)


def _swa_kernel(q_ref, k_ref, v_ref, posq_ref, posk_ref, tab_ref, sink_ref, o_ref, *, tq):
    i = pl.program_id(1)
    w = WINDOW
    dist = lax.broadcasted_iota(jnp.int32, (SUBLANES, LANES), 1)
    bucket = _t5_bucket(dist)
    luts = []
    for hd in range(SWA_HEADS):
        lut = jnp.zeros((SUBLANES, LANES), F32)
        for kk in range(NUM_BUCKETS):
            lut = jnp.where(bucket == kk, tab_ref[kk, hd], lut)
        luts.append(jnp.broadcast_to(lut[0:1, :], (w, LANES)))
    lane = lax.broadcasted_iota(jnp.int32, (w, LANES), 1)
    low = lane < HEAD_DIM
    qrow = lax.broadcasted_iota(jnp.int32, (w, 2 * w), 0)
    kcol = lax.broadcasted_iota(jnp.int32, (w, 2 * w), 1)
    for blk in range(tq // w):
        q0 = (i * (tq // w) + blk) * w
        ks = pl.multiple_of(jnp.maximum(q0 - w, 0), w)
        kb = k_ref[pl.ds(ks, 2 * w), :]
        vb = v_ref[pl.ds(ks, 2 * w), :]
        rel = posq_ref[blk * w:(blk + 1) * w, :] - posk_ref[0, :, pl.ds(ks, 2 * w)]
        idx = jnp.clip(rel, 0, LANES - 1)
        d = (q0 + qrow) - (ks + kcol)
        valid = (d >= 0) & (d < w)
        for j in range(SWA_W // LANES):
            qs = q_ref[blk * w:(blk + 1) * w, j * LANES:(j + 1) * LANES]
            halves = []
            for half in range(2):
                hd = SWA_HEAD_ORDER[2 * j + half]
                qm = jnp.where(low, qs, jnp.zeros_like(qs)) if half == 0 else jnp.where(low, jnp.zeros_like(qs), qs)
                s = _dot_nt(qm, kb)
                bias = jnp.concatenate(
                    [jnp.take_along_axis(luts[hd], idx[:, :LANES], axis=1),
                     jnp.take_along_axis(luts[hd], idx[:, LANES:], axis=1)], axis=1)
                s = jnp.where(valid, s + bias, NEG_BIG)
                sink = sink_ref[hd]
                m = jnp.maximum(jnp.max(s, axis=-1, keepdims=True), sink)
                p = jnp.exp(s - m)
                den = jnp.sum(p, axis=-1, keepdims=True) + jnp.exp(sink - m)
                halves.append(_dot((p * (1.0 / den)).astype(BF16), vb))
            o_ref[blk * w:(blk + 1) * w, j * LANES:(j + 1) * LANES] = jnp.where(low, halves[0], halves[1]).astype(BF16)


def _swa_attention(q, k, v, positions, rel_bias, sinks, *, batch, seq, tq):
    t = q.shape[0]
    nq = seq // tq
    return pl.pallas_call(
        functools.partial(_swa_kernel, tq=tq),
        grid=(batch, nq),
        in_specs=[
            pl.BlockSpec((tq, SWA_W), lambda b, i: (b * nq + i, 0)),
            pl.BlockSpec((seq, LANES), lambda b, i: (b, 0)),
            pl.BlockSpec((seq, LANES), lambda b, i: (b, 0)),
            pl.BlockSpec((tq, 1), lambda b, i: (b * nq + i, 0)),
            pl.BlockSpec((1, 1, seq), lambda b, i: (b, 0, 0)),
            pl.BlockSpec(memory_space=pltpu.SMEM),
            pl.BlockSpec(memory_space=pltpu.SMEM),
        ],
        out_specs=pl.BlockSpec((tq, SWA_W), lambda b, i: (b * nq + i, 0)),
        out_shape=jax.ShapeDtypeStruct((t, SWA_W), BF16),
        compiler_params=_cparams(),
        name="swa_attention",
    )(q, k, v, positions.reshape(t, 1), positions.reshape(batch, 1, seq), rel_bias, sinks)


def _sb_kernel(q_ref, k_ref, v_ref, u_ref, o_ref, z_scr, p_scr, acc_scr, c_scr, *, blk):
    i = pl.program_id(2)
    lane = lax.broadcasted_iota(jnp.int32, (blk, LANES), 1)
    low = lane < HEAD_DIM
    row = lax.broadcasted_iota(jnp.int32, (blk, blk), 0)
    col = lax.broadcasted_iota(jnp.int32, (blk, blk), 1)
    strict = col < row
    q = q_ref[...]
    zero = jnp.zeros_like(q)
    qs = (jnp.where(low, q, zero), jnp.where(low, zero, q))
    heads = range(2)

    def rows(ref, j):
        return ref[pl.ds(pl.multiple_of(j * blk, blk), blk), :]

    def scores(j, slot):
        kj = rows(k_ref, j)
        for hd in heads:
            z_scr[slot, hd] = _dot_nt(qs[hd], kj)

    def accumulate(j, slot):
        vj = rows(v_ref, j)
        return tuple(acc_scr[hd] + _dot(p_scr[slot, hd], vj) for hd in heads)

    def weights(src, dst, masked):
        sps = []
        for hd in heads:
            z = z_scr[src, hd]
            sp = jnp.maximum(z, 0.0) + jnp.log(1.0 + jnp.exp2(jnp.abs(z) * (-LOG2E)))
            if masked:
                sp = jnp.where(strict, sp, 0.0)
            sps.append(sp)
        tri = u_ref[...]
        later = [_dot(sps[hd].astype(BF16), tri) + c_scr[hd] for hd in heads]
        for hd in heads:
            a = jnp.exp((z_scr[src, hd] - sps[hd]) - later[hd])
            if masked:
                a = jnp.where(strict, a, 0.0)
            p_scr[dst, hd] = a.astype(BF16)
        for hd in heads:
            c_scr[hd] += jnp.sum(sps[hd], axis=-1, keepdims=True)

    def step(b, src, dst):
        scores(jnp.maximum(b - 1, 0), dst)
        accs = accumulate(b + 1, src)
        for hd in heads:
            acc_scr[hd] = accs[hd]
        weights(src, dst, False)

    scores(i, 0)
    scores(jnp.maximum(i - 1, 0), 1)
    acc_scr[...] = jnp.zeros_like(acc_scr)
    c_scr[...] = jnp.zeros_like(c_scr)
    weights(0, 1, True)

    def body(t, carry):
        b = i - 1 - 2 * t
        step(b, 1, 0)
        step(b - 1, 0, 1)
        return carry

    lax.fori_loop(0, i // 2, body, 0)

    def finish(slot):
        a0, a1 = accumulate(0, slot)
        o_ref[...] = jnp.where(low, a0, a1).astype(BF16)

    @pl.when(i % 2 == 1)
    def _():
        step(0, 1, 0)
        finish(0)

    @pl.when(i % 2 == 0)
    def _():
        finish(1)


def _sb_attention(q, k, v, *, batch, seq, blk):
    t = q.shape[0]
    nq = seq // blk
    pairs = SB_HEADS // 2
    tri = jnp.tril(jnp.ones((blk, blk), BF16), k=-1)
    return pl.pallas_call(
        functools.partial(_sb_kernel, blk=blk),
        grid=(batch, pairs, nq),
        in_specs=[
            pl.BlockSpec((blk, LANES), lambda b, p, i: (b * nq + i, p)),
            pl.BlockSpec((seq, LANES), lambda b, p, i: (b, p)),
            pl.BlockSpec((seq, LANES), lambda b, p, i: (b, p)),
            _resident((blk, blk), lambda b, p, i: (0, 0)),
        ],
        out_specs=pl.BlockSpec((blk, LANES), lambda b, p, i: (b * nq + i, p)),
        out_shape=jax.ShapeDtypeStruct((t, SB_WIDTH), BF16),
        scratch_shapes=[
            pltpu.VMEM((2, 2, blk, blk), F32),
            pltpu.VMEM((2, 2, blk, blk), BF16),
            pltpu.VMEM((2, blk, LANES), F32),
            pltpu.VMEM((2, blk, 1), F32),
        ],
        compiler_params=_cparams(),
        name="sb_attention",
    )(q, k, v, tri)


def _mix_kernel(x_ref, oa_ref, ob_ref, oc_ref, gt_ref, na_ref, nb_ref, nc_ref, wa_ref, wb_ref, wc_ref, o_ref, *,
                tiles_per_batch):
    b = pl.program_id(0) // tiles_per_batch

    def group(o_ref_, n_ref_):
        o = o_ref_[...].astype(F32)
        return (o * lax.rsqrt(jnp.mean(o * o, axis=-1, keepdims=True) + EPS) * n_ref_[...]).astype(BF16)

    y = (_dot(group(oa_ref, na_ref), wa_ref[...]) + _dot(group(ob_ref, nb_ref), wb_ref[...])
         + _dot(group(oc_ref, nc_ref), wc_ref[...]))
    o_ref[...] = x_ref[...] + gt_ref[0, pl.ds(b, 1), :] * y


def _mix(x, oa, ob, oc, mod, layer, p, *, seq, tm):
    t, d = x.shape
    rows = mod.shape[1]
    const = lambda shape: _resident(shape, lambda i: (0,) * len(shape))
    tok = lambda w: pl.BlockSpec((tm, w), lambda i: (i, 0))
    return pl.pallas_call(
        functools.partial(_mix_kernel, tiles_per_batch=seq // tm),
        grid=(t // tm,),
        in_specs=[
            tok(d), tok(MLA_VW), tok(SWA_W), tok(SB_WIDTH),
            pl.BlockSpec((1, rows, d), lambda i: (layer, 0, 5)),
            const((1, MLA_VW)), const((1, SWA_W)), const((1, SB_WIDTH)),
            const((MLA_VW, d)), const((SWA_W, d)), const((SB_WIDTH, d)),
        ],
        out_specs=tok(d),
        out_shape=jax.ShapeDtypeStruct((t, d), F32),
        compiler_params=_cparams(),
        name="mix_out_proj",
    )(x, oa, ob, oc, mod, p["on_a"], p["on_b"], p["on_c"], p["w_out_a"], p["w_out_b"], p["w_out_c"])


def _proj_columns():
    idx = np.full((PROJ_WIDTH,), -1, np.int64)
    o_ckv = Q_LORA
    o_kpe = o_ckv + KV_LORA
    o_qs = o_kpe + MLA_ROPE
    o_ks = o_qs + SWA_HEADS * HEAD_DIM
    o_vs = o_ks + SWA_KV_HEADS * HEAD_DIM
    o_sb = o_vs + SWA_KV_HEADS * HEAD_DIM
    idx[PROJ_CQ:PROJ_CQ + Q_LORA] = np.arange(Q_LORA)
    idx[PROJ_CKV:PROJ_CKV + KV_LORA] = o_ckv + np.arange(KV_LORA)
    idx[PROJ_KPE + MLA_NOPE:PROJ_KPE + MLA_QK] = o_kpe + np.arange(MLA_ROPE)
    for slot, hd in enumerate(SWA_HEAD_ORDER):
        idx[PROJ_SWAQ + slot * HEAD_DIM:PROJ_SWAQ + (slot + 1) * HEAD_DIM] = o_qs + hd * HEAD_DIM + np.arange(HEAD_DIM)
    idx[PROJ_SWAK:PROJ_SWAK + LANES] = o_ks + np.arange(LANES)
    idx[PROJ_SWAV:PROJ_SWAV + LANES] = o_vs + np.arange(LANES)
    idx[PROJ_SB:PROJ_SB + 3 * SB_WIDTH] = o_sb + np.arange(3 * SB_WIDTH)
    return idx


def _take_columns(w, idx):
    w_ext = jnp.concatenate([w, jnp.zeros((w.shape[0], 1), w.dtype)], axis=1)
    return w_ext[:, np.where(idx < 0, w.shape[1], idx)]


def _layer_params(l, w_in, q_a_norm, kv_a_norm, w_uq, w_ukv, mla_q_norm, mla_k_norm, swa_q_norm, swa_k_norm,
                  sinks, out_norm, w_out):
    uq_idx = np.full((MLA_SLABS,), -1, np.int64)
    kn_idx = np.full((MLA_SLABS,), -1, np.int64)
    v_idx = np.zeros((MLA_VW,), np.int64)
    for hd in range(MLA_HEADS):
        uq_idx[hd * LANES:hd * LANES + MLA_QK] = hd * MLA_QK + np.arange(MLA_QK)
        kn_idx[hd * LANES:hd * LANES + MLA_NOPE] = hd * (MLA_NOPE + MLA_V) + np.arange(MLA_NOPE)
        v_idx[hd * MLA_V:(hd + 1) * MLA_V] = hd * (MLA_NOPE + MLA_V) + MLA_NOPE + np.arange(MLA_V)
    pad = lambda g: jnp.zeros((1, LANES), F32).at[0, :MLA_QK].set(g)
    pair = lambda g: jnp.concatenate([g, g]).reshape(1, LANES)
    swa_rows = np.concatenate([MLA_VW + hd * HEAD_DIM + np.arange(HEAD_DIM) for hd in SWA_HEAD_ORDER])
    return {
        "w_in": _take_columns(w_in[l], _proj_columns()).astype(BF16),
        "q_a_norm": q_a_norm[l].reshape(1, Q_LORA),
        "kv_a_norm": kv_a_norm[l].reshape(1, KV_LORA),
        "w_uq": _take_columns(w_uq[l], uq_idx).astype(BF16),
        "w_kn": _take_columns(w_ukv[l], kn_idx).astype(BF16),
        "w_v": w_ukv[l][:, v_idx].astype(BF16),
        "mla_q_norm": pad(mla_q_norm[l]),
        "mla_k_norm": pad(mla_k_norm[l]),
        "swa_q_norm": pair(swa_q_norm[l]),
        "swa_k_norm": pair(swa_k_norm[l]),
        "sinks": sinks[l],
        "on_a": out_norm[l][:MLA_VW].reshape(1, MLA_VW),
        "on_b": out_norm[l][swa_rows].reshape(1, SWA_W),
        "on_c": out_norm[l][MLA_VW + SWA_W:].reshape(1, SB_WIDTH),
        "w_out_a": w_out[l][:MLA_VW].astype(BF16),
        "w_out_b": w_out[l][swa_rows].astype(BF16),
        "w_out_c": w_out[l][MLA_VW + SWA_W:].astype(BF16),
    }


def kernel(x, c, positions, rel_bias, norm_g, w_mod, b_mod, w_ffn1_gu, w_ffn1_down, w_in, q_a_norm, kv_a_norm,
           w_uq, w_ukv, mla_q_norm, mla_k_norm, swa_q_norm, swa_k_norm, sinks, out_norm, w_out, w_ffn2_gu,
           w_ffn2_down):
    batch, seq, d = x.shape
    depth = w_mod.shape[0]
    t = batch * seq
    tm = min(512, seq)
    mla_blk = min(512, seq)
    sb_blk = min(256, seq)
    swa_tq = min(512, seq)

    xt = x.reshape(t, d)
    mod = _modulation(c, w_mod, b_mod)
    gains = norm_g.reshape(depth * 3, 1, d)
    cos_t, sin_t = _rope_tables(positions, tm)
    for l in range(depth):
        p = _layer_params(l, w_in, q_a_norm, kv_a_norm, w_uq, w_ukv, mla_q_norm, mla_k_norm, swa_q_norm,
                          swa_k_norm, sinks, out_norm, w_out)
        xt = _ffn(xt, mod, gains, l, 0, w_ffn1_gu[l], w_ffn1_down[l], seq=seq, tm=tm)
        mq, mk, mv, sq, sk, sv, bq, bk, bv = _prologue(xt, mod, gains, l, p, cos_t, sin_t, seq=seq, tm=tm)
        oa = _mla_attention(mq, mk, mv, batch=batch, seq=seq, blk=mla_blk)
        ob = _swa_attention(sq, sk, sv, positions, rel_bias, p["sinks"], batch=batch, seq=seq, tq=swa_tq)
        oc = _sb_attention(bq, bk, bv, batch=batch, seq=seq, blk=sb_blk)
        xt = _mix(xt, oa, ob, oc, mod, l, p, seq=seq, tm=tm)
        xt = _ffn(xt, mod, gains, l, 2, w_ffn2_gu[l], w_ffn2_down[l], seq=seq, tm=tm)
    return xt.reshape(batch, seq, d)
```

```python
import functools
import math

import numpy as np
import jax
import jax.numpy as jnp
from jax import lax
from jax.experimental import pallas as pl
from jax.experimental.pallas import tpu as pltpu

HEAD_DIM = 64
MLA_HEADS = 6
MLA_NOPE = 64
MLA_ROPE = 32
MLA_V = 64
MLA_QK = MLA_NOPE + MLA_ROPE
Q_LORA = 256
KV_LORA = 128
SWA_HEADS = 6
SWA_KV_HEADS = 2
SWA_GROUP = SWA_HEADS // SWA_KV_HEADS
WINDOW = 128
SB_HEADS = 4
NUM_BUCKETS = 32
MAX_DISTANCE = 128
ROPE_THETA = 10000.0
EPS = 1e-6

LANES = 128
SUBLANES = 8
VMEM_LIMIT_BYTES = 56 * 2**20

F32 = jnp.float32
BF16 = jnp.bfloat16
NEG_BIG = -1e30
LOG2E = math.log2(math.e)

PROJ_CQ = 0
PROJ_CKV = PROJ_CQ + Q_LORA
PROJ_KPE = PROJ_CKV + KV_LORA
PROJ_SWAQ = PROJ_KPE + LANES
PROJ_SWAK = PROJ_SWAQ + SWA_HEADS * HEAD_DIM
PROJ_SWAV = PROJ_SWAK + LANES
PROJ_SB = PROJ_SWAV + LANES
SB_WIDTH = SB_HEADS * HEAD_DIM
PROJ_WIDTH = PROJ_SB + 3 * SB_WIDTH
MLA_SLABS = MLA_HEADS * LANES
MLA_VW = MLA_HEADS * MLA_V
SWA_W = SWA_HEADS * HEAD_DIM
SWA_HEAD_ORDER = (0, 3, 1, 4, 2, 5)


def _cparams():
    return pltpu.CompilerParams(vmem_limit_bytes=VMEM_LIMIT_BYTES)


def _resident(shape, index_map):
    return pl.BlockSpec(shape, index_map, pipeline_mode=pl.Buffered(1))


def _sigmoid(x):
    return 1.0 / (1.0 + jnp.exp(-x))


def _modulated_norm(x, gain, shift, scale):
    ms = jnp.mean(x * x, axis=-1, keepdims=True)
    y = x * lax.rsqrt(ms + EPS) * gain
    return y * (1.0 + scale) + shift


def _dot(a, b):
    return jnp.dot(a, b, preferred_element_type=F32)


def _dot_nt(a, b):
    return lax.dot_general(a, b, (((1,), (1,)), ((), ())), preferred_element_type=F32)


def _mod_kernel(c_ref, w_ref, b_ref, o_ref):
    c = c_ref[...]
    cond = c * _sigmoid(c)
    o_ref[0] = jnp.dot(cond, w_ref[0], preferred_element_type=F32,
                       precision=lax.Precision.HIGHEST) + b_ref[0]


def _modulation(c, w_mod, b_mod):
    depth, d, n = w_mod.shape
    b = c.shape[0]
    rows = SUBLANES
    c_pad = jnp.zeros((rows, d), F32).at[:b].set(c)
    bn = n // 8
    return pl.pallas_call(
        _mod_kernel,
        grid=(depth, n // bn),
        in_specs=[
            pl.BlockSpec((rows, d), lambda l, j: (0, 0)),
            pl.BlockSpec((1, d, bn), lambda l, j: (l, 0, j)),
            pl.BlockSpec((1, 1, bn), lambda l, j: (l, 0, j)),
        ],
        out_specs=pl.BlockSpec((1, rows, bn), lambda l, j: (l, 0, j)),
        out_shape=jax.ShapeDtypeStruct((depth, rows, n), F32),
        compiler_params=_cparams(),
        name="modulation",
    )(c_pad, w_mod, b_mod.reshape(depth, 1, n))


def _rope_kernel(pos_ref, freq_ref, cos_ref, sin_ref):
    ang = pos_ref[...].astype(F32) * freq_ref[...]
    lane = lax.broadcasted_iota(jnp.int32, ang.shape, 1)
    lo = MLA_NOPE
    mid = MLA_NOPE + MLA_ROPE // 2
    hi = MLA_NOPE + MLA_ROPE
    sn = jnp.sin(ang)
    cos_ref[...] = jnp.where(lane < lo, 1.0, jnp.where(lane < hi, jnp.cos(ang), 0.0))
    sin_ref[...] = jnp.where(lane < lo, 0.0, jnp.where(lane < mid, -sn, jnp.where(lane < hi, sn, 0.0)))


def _rope_tables(positions, tm):
    t = positions.size
    half = MLA_ROPE // 2
    freqs = ROPE_THETA ** (-jnp.arange(half, dtype=F32) / half)
    row = jnp.zeros((1, LANES), F32)
    row = row.at[0, MLA_NOPE:MLA_NOPE + half].set(freqs).at[0, MLA_NOPE + half:MLA_QK].set(freqs)
    return pl.pallas_call(
        _rope_kernel,
        grid=(t // tm,),
        in_specs=[pl.BlockSpec((tm, 1), lambda i: (i, 0)), pl.BlockSpec((1, LANES), lambda i: (0, 0))],
        out_specs=[pl.BlockSpec((tm, LANES), lambda i: (i, 0))] * 2,
        out_shape=[jax.ShapeDtypeStruct((t, LANES), F32)] * 2,
        compiler_params=_cparams(),
        name="rope_tables",
    )(positions.reshape(t, 1), row)


def _rope(x, cosv, sinv, lane):
    mid = MLA_NOPE + MLA_ROPE // 2
    half = MLA_ROPE // 2
    swapped = jnp.where(lane < mid, pltpu.roll(x, LANES - half, 1), pltpu.roll(x, half, 1))
    return x * cosv + swapped * sinv


def _ffn_kernel(x_ref, sh_ref, sc_ref, gt_ref, g_ref, wgu_ref, wd_ref, o_ref, act_ref, *,
                tiles_per_batch, d_ff, chunk):
    b = pl.program_id(0) // tiles_per_batch
    x = x_ref[...]
    h = _modulated_norm(x, g_ref[0], sh_ref[0, pl.ds(b, 1), :], sc_ref[0, pl.ds(b, 1), :])
    hb = h.astype(BF16)
    for c in range(d_ff // chunk):
        g = _dot(hb, wgu_ref[:, c * chunk:(c + 1) * chunk])
        u = _dot(hb, wgu_ref[:, d_ff + c * chunk:d_ff + (c + 1) * chunk])
        act_ref[:, c * chunk:(c + 1) * chunk] = (g * _sigmoid(g) * u).astype(BF16)
    y = _dot(act_ref[...], wd_ref[...])
    o_ref[...] = x + (0.5 * gt_ref[0, pl.ds(b, 1), :]) * y


def _ffn(x, mod, gains, layer, which, w_gu, w_down, *, seq, tm):
    t, d = x.shape
    d_ff = w_down.shape[0]
    chunk = 2 * LANES
    rows = mod.shape[1]
    mod_spec = lambda k: pl.BlockSpec((1, rows, d), lambda i: (layer, 0, 3 * which + k))
    return pl.pallas_call(
        functools.partial(_ffn_kernel, tiles_per_batch=seq // tm, d_ff=d_ff, chunk=chunk),
        grid=(t // tm,),
        in_specs=[
            pl.BlockSpec((tm, d), lambda i: (i, 0)),
            mod_spec(0), mod_spec(1), mod_spec(2),
            pl.BlockSpec((1, 1, d), lambda i: (layer * 3 + which, 0, 0)),
            _resident((d, 2 * d_ff), lambda i: (0, 0)),
            _resident((d_ff, d), lambda i: (0, 0)),
        ],
        out_specs=pl.BlockSpec((tm, d), lambda i: (i, 0)),
        out_shape=jax.ShapeDtypeStruct((t, d), F32),
        scratch_shapes=[pltpu.VMEM((tm, d_ff), BF16)],
        compiler_params=_cparams(),
        name=f"ffn{which}",
    )(x, mod, mod, mod, gains, w_gu.astype(BF16), w_down.astype(BF16))


def _prologue_kernel(x_ref, sh_ref, sc_ref, g_ref, win_ref, qan_ref, kvan_ref, wuq_ref, wkn_ref, wv_ref,
                     qn_ref, kn_ref, sqn_ref, skn_ref, cos_ref, sin_ref,
                     mq_ref, mk_ref, mv_ref, sq_ref, sk_ref, sv_ref, bq_ref, bk_ref, bv_ref, *,
                     tiles_per_batch):
    b = pl.program_id(0) // tiles_per_batch
    x = x_ref[...]
    h = _modulated_norm(x, g_ref[0], sh_ref[0, pl.ds(b, 1), :], sc_ref[0, pl.ds(b, 1), :])
    proj = _dot(h.astype(BF16), win_ref[...])
    tm = x.shape[0]
    lane = lax.broadcasted_iota(jnp.int32, (tm, LANES), 1)
    low = lane < HEAD_DIM

    def rms(v, width):
        return v * lax.rsqrt(jnp.sum(v * v, axis=-1, keepdims=True) * (1.0 / width) + EPS)

    cq = rms(proj[:, PROJ_CQ:PROJ_CQ + Q_LORA], Q_LORA) * qan_ref[...]
    ckv = (rms(proj[:, PROJ_CKV:PROJ_CKV + KV_LORA], KV_LORA) * kvan_ref[...]).astype(BF16)
    q_pre = _dot(cq.astype(BF16), wuq_ref[...])
    k_pre = _dot(ckv, wkn_ref[...])
    mv_ref[...] = _dot(ckv, wv_ref[...]).astype(BF16)
    kpe = proj[:, PROJ_KPE:PROJ_KPE + LANES]
    kpe_ss = jnp.sum(kpe * kpe, axis=-1, keepdims=True)
    cosv = cos_ref[...]
    sinv = sin_ref[...]
    qn = qn_ref[...]
    kn = kn_ref[...]
    q_scale = MLA_QK ** -0.5 * LOG2E
    for hd in range(MLA_HEADS):
        sl = slice(hd * LANES, (hd + 1) * LANES)
        qh = q_pre[:, sl]
        rq = lax.rsqrt(jnp.sum(qh * qh, axis=-1, keepdims=True) * (1.0 / MLA_QK) + EPS)
        mq_ref[:, sl] = (_rope(qh * rq * qn, cosv, sinv, lane) * q_scale).astype(BF16)
        kh = k_pre[:, sl]
        rk = lax.rsqrt((jnp.sum(kh * kh, axis=-1, keepdims=True) + kpe_ss) * (1.0 / MLA_QK) + EPS)
        mk_ref[:, sl] = _rope((kh + kpe) * rk * kn, cosv, sinv, lane).astype(BF16)

    def pair_norm(v, gain):
        sq = v * v
        lo = jnp.sum(jnp.where(low, sq, 0.0), axis=-1, keepdims=True)
        hi = jnp.sum(jnp.where(low, 0.0, sq), axis=-1, keepdims=True)
        r = jnp.where(low, lax.rsqrt(lo * (1.0 / HEAD_DIM) + EPS), lax.rsqrt(hi * (1.0 / HEAD_DIM) + EPS))
        return v * r * gain

    sqn = sqn_ref[...]
    for j in range(SWA_W // LANES):
        sl = slice(j * LANES, (j + 1) * LANES)
        v = proj[:, PROJ_SWAQ + j * LANES:PROJ_SWAQ + (j + 1) * LANES]
        sq_ref[:, sl] = (pair_norm(v, sqn) * (HEAD_DIM ** -0.5)).astype(BF16)
    sk_ref[...] = pair_norm(proj[:, PROJ_SWAK:PROJ_SWAK + LANES], skn_ref[...]).astype(BF16)
    sv_ref[...] = proj[:, PROJ_SWAV:PROJ_SWAV + LANES].astype(BF16)

    bq_ref[...] = (proj[:, PROJ_SB:PROJ_SB + SB_WIDTH] * (HEAD_DIM ** -0.5)).astype(BF16)
    bk_ref[...] = proj[:, PROJ_SB + SB_WIDTH:PROJ_SB + 2 * SB_WIDTH].astype(BF16)
    bv_ref[...] = proj[:, PROJ_SB + 2 * SB_WIDTH:PROJ_SB + 3 * SB_WIDTH].astype(BF16)


def _prologue(x, mod, gains, layer, p, cos_t, sin_t, *, seq, tm):
    t, d = x.shape
    rows = mod.shape[1]
    mod_spec = lambda k: pl.BlockSpec((1, rows, d), lambda i: (layer, 0, 3 + k))
    const = lambda shape: _resident(shape, lambda i: (0,) * len(shape))
    tok = lambda w: pl.BlockSpec((tm, w), lambda i: (i, 0))
    widths = (MLA_SLABS, MLA_SLABS, MLA_VW, SWA_W, LANES, LANES, SB_WIDTH, SB_WIDTH, SB_WIDTH)
    return pl.pallas_call(
        functools.partial(_prologue_kernel, tiles_per_batch=seq // tm),
        grid=(t // tm,),
        in_specs=[
            tok(d), mod_spec(0), mod_spec(1),
            pl.BlockSpec((1, 1, d), lambda i: (layer * 3 + 1, 0, 0)),
            const((d, PROJ_WIDTH)), const((1, Q_LORA)), const((1, KV_LORA)),
            const((Q_LORA, MLA_SLABS)), const((KV_LORA, MLA_SLABS)), const((KV_LORA, MLA_VW)),
            const((1, LANES)), const((1, LANES)), const((1, LANES)), const((1, LANES)),
            tok(LANES), tok(LANES),
        ],
        out_specs=[tok(w) for w in widths],
        out_shape=[jax.ShapeDtypeStruct((t, w), BF16) for w in widths],
        compiler_params=_cparams(),
        name="attn_prologue",
    )(x, mod, mod, gains, p["w_in"], p["q_a_norm"], p["kv_a_norm"], p["w_uq"], p["w_kn"], p["w_v"],
      p["mla_q_norm"], p["mla_k_norm"], p["swa_q_norm"], p["swa_k_norm"], cos_t, sin_t)


def _mla_kernel(q_ref, k_ref, v_ref, o_ref, s_scr, pmax_scr, p_scr, acc_scr, m_scr, l_scr, alpha_scr, *, blk):
    i = pl.program_id(2)
    lane = lax.broadcasted_iota(jnp.int32, (blk, LANES), 1)
    row = lax.broadcasted_iota(jnp.int32, (blk, blk), 0)
    col = lax.broadcasted_iota(jnp.int32, (blk, blk), 1)
    causal = col <= row
    qs = (q_ref[:, 0:LANES], q_ref[:, LANES:2 * LANES])
    heads = range(2)

    def rows(ref, j):
        return ref[pl.ds(pl.multiple_of(j * blk, blk), blk), :]

    chunks = [slice(c * LANES, (c + 1) * LANES) for c in range(blk // LANES)]

    def scores(j, slot, masked):
        kj = rows(k_ref, j)
        for hd in heads:
            s = _dot_nt(qs[hd], kj[:, hd * LANES:(hd + 1) * LANES])
            if masked:
                s = jnp.where(causal, s, NEG_BIG)
            s_scr[slot, hd] = s
            pm = s[:, chunks[0]]
            for sl in chunks[1:]:
                pm = jnp.maximum(pm, s[:, sl])
            pmax_scr[slot, hd] = pm

    def accumulate(j, slot):
        vj = rows(v_ref, j)
        return tuple(alpha_scr[hd] * acc_scr[hd] + _dot(p_scr[slot, hd], vj) for hd in heads)

    def softmax(src, dst):
        m_new = [jnp.maximum(m_scr[hd], jnp.max(pmax_scr[src, hd], axis=-1, keepdims=True)) for hd in heads]
        for hd in heads:
            psum = None
            for sl in chunks:
                p = jnp.exp2(s_scr[src, hd, :, sl] - m_new[hd])
                p_scr[dst, hd, :, sl] = p.astype(BF16)
                psum = p if psum is None else psum + p
            alpha = jnp.exp2(m_scr[hd] - m_new[hd])
            l_scr[hd] = alpha * l_scr[hd] + jnp.sum(psum, axis=-1, keepdims=True)
            alpha_scr[hd] = alpha
            m_scr[hd] = m_new[hd]

    def step(b, src, dst):
        scores(jnp.maximum(b - 1, 0), dst, False)
        accs = accumulate(b + 1, src)
        for hd in heads:
            acc_scr[hd] = accs[hd]
        softmax(src, dst)

    scores(i, 0, True)
    scores(jnp.maximum(i - 1, 0), 1, False)
    acc_scr[...] = jnp.zeros_like(acc_scr)
    l_scr[...] = jnp.zeros_like(l_scr)
    m_scr[...] = jnp.full(m_scr.shape, NEG_BIG, F32)
    softmax(0, 1)

    def body(t, carry):
        b = i - 1 - 2 * t
        step(b, 1, 0)
        step(b - 1, 0, 1)
        return carry

    lax.fori_loop(0, i // 2, body, 0)

    def finish(slot):
        a0, a1 = accumulate(0, slot)
        o_ref[...] = jnp.where(lane < MLA_V, a0 * (1.0 / l_scr[0]), a1 * (1.0 / l_scr[1])).astype(BF16)

    @pl.when(i % 2 == 1)
    def _():
        step(0, 1, 0)
        finish(0)

    @pl.when(i % 2 == 0)
    def _():
        finish(1)


def _mla_attention(q, k, v, *, batch, seq, blk):
    t = q.shape[0]
    nq = seq // blk
    pairs = MLA_HEADS // 2
    return pl.pallas_call(
        functools.partial(_mla_kernel, blk=blk),
        grid=(batch, pairs, nq),
        in_specs=[
            pl.BlockSpec((blk, 2 * LANES), lambda b, p, i: (b * nq + i, p)),
            pl.BlockSpec((seq, 2 * LANES), lambda b, p, i: (b, p)),
            pl.BlockSpec((seq, LANES), lambda b, p, i: (b, p)),
        ],
        out_specs=pl.BlockSpec((blk, LANES), lambda b, p, i: (b * nq + i, p)),
        out_shape=jax.ShapeDtypeStruct((t, MLA_VW), BF16),
        scratch_shapes=[
            pltpu.VMEM((2, 2, blk, blk), F32),
            pltpu.VMEM((2, 2, blk, LANES), F32),
            pltpu.VMEM((2, 2, blk, blk), BF16),
            pltpu.VMEM((2, blk, LANES), F32),
            pltpu.VMEM((2, blk, LANES), F32),
            pltpu.VMEM((2, blk, LANES), F32),
            pltpu.VMEM((2, blk, LANES), F32),
        ],
        compiler_params=_cparams(),
        name="mla_attention",
    )(q, k, v)


def _t5_bucket(rel):
    n = jnp.maximum(rel, 0)
    max_exact = NUM_BUCKETS // 2
    nf = jnp.maximum(n, 1).astype(F32)
    large = max_exact + (jnp.log(nf / max_exact) / math.log(MAX_DISTANCE / max_exact)
                         * (NUM_BUCKETS - max_exact)).astype(jnp.int32)
    large = jnp.minimum(large, NUM_BUCKETS - 1)
    return jnp.where(n < max_exact, n, large)


def _swa_kernel(q_ref, k_ref, v_ref, posq_ref, posk_ref, tab_ref, sink_ref, o_ref, *, tq):
    i = pl.program_id(1)
    w = WINDOW
    dist = lax.broadcasted_iota(jnp.int32, (SUBLANES, LANES), 1)
    bucket = _t5_bucket(dist)
    luts = []
    for hd in range(SWA_HEADS):
        lut = jnp.zeros((SUBLANES, LANES), F32)
        for kk in range(NUM_BUCKETS):
            lut = jnp.where(bucket == kk, tab_ref[kk, hd], lut)
        luts.append(jnp.broadcast_to(lut[0:1, :], (w, LANES)))
    lane = lax.broadcasted_iota(jnp.int32, (w, LANES), 1)
    low = lane < HEAD_DIM
    qrow = lax.broadcasted_iota(jnp.int32, (w, 2 * w), 0)
    kcol = lax.broadcasted_iota(jnp.int32, (w, 2 * w), 1)
    for blk in range(tq // w):
        q0 = (i * (tq // w) + blk) * w
        ks = pl.multiple_of(jnp.maximum(q0 - w, 0), w)
        kb = k_ref[pl.ds(ks, 2 * w), :]
        vb = v_ref[pl.ds(ks, 2 * w), :]
        rel = posq_ref[blk * w:(blk + 1) * w, :] - posk_ref[0, :, pl.ds(ks, 2 * w)]
        idx = jnp.clip(rel, 0, LANES - 1)
        d = (q0 + qrow) - (ks + kcol)
        valid = (d >= 0) & (d < w)
        for j in range(SWA_W // LANES):
            qs = q_ref[blk * w:(blk + 1) * w, j * LANES:(j + 1) * LANES]
            halves = []
            for half in range(2):
                hd = SWA_HEAD_ORDER[2 * j + half]
                qm = jnp.where(low, qs, jnp.zeros_like(qs)) if half == 0 else jnp.where(low, jnp.zeros_like(qs), qs)
                s = _dot_nt(qm, kb)
                bias = jnp.concatenate(
                    [jnp.take_along_axis(luts[hd], idx[:, :LANES], axis=1),
                     jnp.take_along_axis(luts[hd], idx[:, LANES:], axis=1)], axis=1)
                s = jnp.where(valid, s + bias, NEG_BIG)
                sink = sink_ref[hd]
                m = jnp.maximum(jnp.max(s, axis=-1, keepdims=True), sink)
                p = jnp.exp(s - m)
                den = jnp.sum(p, axis=-1, keepdims=True) + jnp.exp(sink - m)
                halves.append(_dot((p * (1.0 / den)).astype(BF16), vb))
            o_ref[blk * w:(blk + 1) * w, j * LANES:(j + 1) * LANES] = jnp.where(low, halves[0], halves[1]).astype(BF16)


def _swa_attention(q, k, v, positions, rel_bias, sinks, *, batch, seq, tq):
    t = q.shape[0]
    nq = seq // tq
    return pl.pallas_call(
        functools.partial(_swa_kernel, tq=tq),
        grid=(batch, nq),
        in_specs=[
            pl.BlockSpec((tq, SWA_W), lambda b, i: (b * nq + i, 0)),
            pl.BlockSpec((seq, LANES), lambda b, i: (b, 0)),
            pl.BlockSpec((seq, LANES), lambda b, i: (b, 0)),
            pl.BlockSpec((tq, 1), lambda b, i: (b * nq + i, 0)),
            pl.BlockSpec((1, 1, seq), lambda b, i: (b, 0, 0)),
            pl.BlockSpec(memory_space=pltpu.SMEM),
            pl.BlockSpec(memory_space=pltpu.SMEM),
        ],
        out_specs=pl.BlockSpec((tq, SWA_W), lambda b, i: (b * nq + i, 0)),
        out_shape=jax.ShapeDtypeStruct((t, SWA_W), BF16),
        compiler_params=_cparams(),
        name="swa_attention",
    )(q, k, v, positions.reshape(t, 1), positions.reshape(batch, 1, seq), rel_bias, sinks)


def _sb_kernel(q_ref, k_ref, v_ref, u_ref, o_ref, z_scr, p_scr, acc_scr, c_scr, *, blk):
    i = pl.program_id(2)
    lane = lax.broadcasted_iota(jnp.int32, (blk, LANES), 1)
    low = lane < HEAD_DIM
    row = lax.broadcasted_iota(jnp.int32, (blk, blk), 0)
    col = lax.broadcasted_iota(jnp.int32, (blk, blk), 1)
    strict = col < row
    q = q_ref[...]
    zero = jnp.zeros_like(q)
    qs = (jnp.where(low, q, zero), jnp.where(low, zero, q))
    heads = range(2)

    def rows(ref, j):
        return ref[pl.ds(pl.multiple_of(j * blk, blk), blk), :]

    def scores(j, slot):
        kj = rows(k_ref, j)
        for hd in heads:
            z_scr[slot, hd] = _dot_nt(qs[hd], kj)

    def accumulate(j, slot):
        vj = rows(v_ref, j)
        return tuple(acc_scr[hd] + _dot(p_scr[slot, hd], vj) for hd in heads)

    def weights(src, dst, masked):
        sps = []
        for hd in heads:
            z = z_scr[src, hd]
            sp = jnp.maximum(z, 0.0) + jnp.log(1.0 + jnp.exp2(jnp.abs(z) * (-LOG2E)))
            if masked:
                sp = jnp.where(strict, sp, 0.0)
            sps.append(sp)
        tri = u_ref[...]
        later = [_dot(sps[hd].astype(BF16), tri) + c_scr[hd] for hd in heads]
        for hd in heads:
            a = jnp.exp((z_scr[src, hd] - sps[hd]) - later[hd])
            if masked:
                a = jnp.where(strict, a, 0.0)
            p_scr[dst, hd] = a.astype(BF16)
        for hd in heads:
            c_scr[hd] += jnp.sum(sps[hd], axis=-1, keepdims=True)

    def step(b, src, dst):
        scores(jnp.maximum(b - 1, 0), dst)
        accs = accumulate(b + 1, src)
        for hd in heads:
            acc_scr[hd] = accs[hd]
        weights(src, dst, False)

    scores(i, 0)
    scores(jnp.maximum(i - 1, 0), 1)
    acc_scr[...] = jnp.zeros_like(acc_scr)
    c_scr[...] = jnp.zeros_like(c_scr)
    weights(0, 1, True)

    def body(t, carry):
        b = i - 1 - 2 * t
        step(b, 1, 0)
        step(b - 1, 0, 1)
        return carry

    lax.fori_loop(0, i // 2, body, 0)

    def finish(slot):
        a0, a1 = accumulate(0, slot)
        o_ref[...] = jnp.where(low, a0, a1).astype(BF16)

    @pl.when(i % 2 == 1)
    def _():
        step(0, 1, 0)
        finish(0)

    @pl.when(i % 2 == 0)
    def _():
        finish(1)


def _sb_attention(q, k, v, *, batch, seq, blk):
    t = q.shape[0]
    nq = seq // blk
    pairs = SB_HEADS // 2
    tri = jnp.tril(jnp.ones((blk, blk), BF16), k=-1)
    return pl.pallas_call(
        functools.partial(_sb_kernel, blk=blk),
        grid=(batch, pairs, nq),
        in_specs=[
            pl.BlockSpec((blk, LANES), lambda b, p, i: (b * nq + i, p)),
            pl.BlockSpec((seq, LANES), lambda b, p, i: (b, p)),
            pl.BlockSpec((seq, LANES), lambda b, p, i: (b, p)),
            _resident((blk, blk), lambda b, p, i: (0, 0)),
        ],
        out_specs=pl.BlockSpec((blk, LANES), lambda b, p, i: (b * nq + i, p)),
        out_shape=jax.ShapeDtypeStruct((t, SB_WIDTH), BF16),
        scratch_shapes=[
            pltpu.VMEM((2, 2, blk, blk), F32),
            pltpu.VMEM((2, 2, blk, blk), BF16),
            pltpu.VMEM((2, blk, LANES), F32),
            pltpu.VMEM((2, blk, 1), F32),
        ],
        compiler_params=_cparams(),
        name="sb_attention",
    )(q, k, v, tri)


def _mix_kernel(x_ref, oa_ref, ob_ref, oc_ref, gt_ref, na_ref, nb_ref, nc_ref, wa_ref, wb_ref, wc_ref, o_ref, *,
                tiles_per_batch):
    b = pl.program_id(0) // tiles_per_batch

    def group(o_ref_, n_ref_):
        o = o_ref_[...].astype(F32)
        return (o * lax.rsqrt(jnp.mean(o * o, axis=-1, keepdims=True) + EPS) * n_ref_[...]).astype(BF16)

    y = (_dot(group(oa_ref, na_ref), wa_ref[...]) + _dot(group(ob_ref, nb_ref), wb_ref[...])
         + _dot(group(oc_ref, nc_ref), wc_ref[...]))
    o_ref[...] = x_ref[...] + gt_ref[0, pl.ds(b, 1), :] * y


def _mix(x, oa, ob, oc, mod, layer, p, *, seq, tm):
    t, d = x.shape
    rows = mod.shape[1]
    const = lambda shape: _resident(shape, lambda i: (0,) * len(shape))
    tok = lambda w: pl.BlockSpec((tm, w), lambda i: (i, 0))
    return pl.pallas_call(
        functools.partial(_mix_kernel, tiles_per_batch=seq // tm),
        grid=(t // tm,),
        in_specs=[
            tok(d), tok(MLA_VW), tok(SWA_W), tok(SB_WIDTH),
            pl.BlockSpec((1, rows, d), lambda i: (layer, 0, 5)),
            const((1, MLA_VW)), const((1, SWA_W)), const((1, SB_WIDTH)),
            const((MLA_VW, d)), const((SWA_W, d)), const((SB_WIDTH, d)),
        ],
        out_specs=tok(d),
        out_shape=jax.ShapeDtypeStruct((t, d), F32),
        compiler_params=_cparams(),
        name="mix_out_proj",
    )(x, oa, ob, oc, mod, p["on_a"], p["on_b"], p["on_c"], p["w_out_a"], p["w_out_b"], p["w_out_c"])


def _proj_columns():
    idx = np.full((PROJ_WIDTH,), -1, np.int64)
    o_ckv = Q_LORA
    o_kpe = o_ckv + KV_LORA
    o_qs = o_kpe + MLA_ROPE
    o_ks = o_qs + SWA_HEADS * HEAD_DIM
    o_vs = o_ks + SWA_KV_HEADS * HEAD_DIM
    o_sb = o_vs + SWA_KV_HEADS * HEAD_DIM
    idx[PROJ_CQ:PROJ_CQ + Q_LORA] = np.arange(Q_LORA)
    idx[PROJ_CKV:PROJ_CKV + KV_LORA] = o_ckv + np.arange(KV_LORA)
    idx[PROJ_KPE + MLA_NOPE:PROJ_KPE + MLA_QK] = o_kpe + np.arange(MLA_ROPE)
    for slot, hd in enumerate(SWA_HEAD_ORDER):
        idx[PROJ_SWAQ + slot * HEAD_DIM:PROJ_SWAQ + (slot + 1) * HEAD_DIM] = o_qs + hd * HEAD_DIM + np.arange(HEAD_DIM)
    idx[PROJ_SWAK:PROJ_SWAK + LANES] = o_ks + np.arange(LANES)
    idx[PROJ_SWAV:PROJ_SWAV + LANES] = o_vs + np.arange(LANES)
    idx[PROJ_SB:PROJ_SB + 3 * SB_WIDTH] = o_sb + np.arange(3 * SB_WIDTH)
    return idx


def _take_columns(w, idx):
    w_ext = jnp.concatenate([w, jnp.zeros((w.shape[0], 1), w.dtype)], axis=1)
    return w_ext[:, np.where(idx < 0, w.shape[1], idx)]


def _layer_params(l, w_in, q_a_norm, kv_a_norm, w_uq, w_ukv, mla_q_norm, mla_k_norm, swa_q_norm, swa_k_norm,
                  sinks, out_norm, w_out):
    uq_idx = np.full((MLA_SLABS,), -1, np.int64)
    kn_idx = np.full((MLA_SLABS,), -1, np.int64)
    v_idx = np.zeros((MLA_VW,), np.int64)
    for hd in range(MLA_HEADS):
        uq_idx[hd * LANES:hd * LANES + MLA_QK] = hd * MLA_QK + np.arange(MLA_QK)
        kn_idx[hd * LANES:hd * LANES + MLA_NOPE] = hd * (MLA_NOPE + MLA_V) + np.arange(MLA_NOPE)
        v_idx[hd * MLA_V:(hd + 1) * MLA_V] = hd * (MLA_NOPE + MLA_V) + MLA_NOPE + np.arange(MLA_V)
    pad = lambda g: jnp.zeros((1, LANES), F32).at[0, :MLA_QK].set(g)
    pair = lambda g: jnp.concatenate([g, g]).reshape(1, LANES)
    swa_rows = np.concatenate([MLA_VW + hd * HEAD_DIM + np.arange(HEAD_DIM) for hd in SWA_HEAD_ORDER])
    return {
        "w_in": _take_columns(w_in[l], _proj_columns()).astype(BF16),
        "q_a_norm": q_a_norm[l].reshape(1, Q_LORA),
        "kv_a_norm": kv_a_norm[l].reshape(1, KV_LORA),
        "w_uq": _take_columns(w_uq[l], uq_idx).astype(BF16),
        "w_kn": _take_columns(w_ukv[l], kn_idx).astype(BF16),
        "w_v": w_ukv[l][:, v_idx].astype(BF16),
        "mla_q_norm": pad(mla_q_norm[l]),
        "mla_k_norm": pad(mla_k_norm[l]),
        "swa_q_norm": pair(swa_q_norm[l]),
        "swa_k_norm": pair(swa_k_norm[l]),
        "sinks": sinks[l],
        "on_a": out_norm[l][:MLA_VW].reshape(1, MLA_VW),
        "on_b": out_norm[l][swa_rows].reshape(1, SWA_W),
        "on_c": out_norm[l][MLA_VW + SWA_W:].reshape(1, SB_WIDTH),
        "w_out_a": w_out[l][:MLA_VW].astype(BF16),
        "w_out_b": w_out[l][swa_rows].astype(BF16),
        "w_out_c": w_out[l][MLA_VW + SWA_W:].astype(BF16),
    }


def kernel(x, c, positions, rel_bias, norm_g, w_mod, b_mod, w_ffn1_gu, w_ffn1_down, w_in, q_a_norm, kv_a_norm,
           w_uq, w_ukv, mla_q_norm, mla_k_norm, swa_q_norm, swa_k_norm, sinks, out_norm, w_out, w_ffn2_gu,
           w_ffn2_down):
    batch, seq, d = x.shape
    depth = w_mod.shape[0]
    t = batch * seq
    tm = min(512, seq)
    mla_blk = min(512, seq)
    sb_blk = min(256, seq)
    swa_tq = min(512, seq)

    xt = x.reshape(t, d)
    mod = _modulation(c, w_mod, b_mod)
    gains = norm_g.reshape(depth * 3, 1, d)
    cos_t, sin_t = _rope_tables(positions, tm)
    for l in range(depth):
        p = _layer_params(l, w_in, q_a_norm, kv_a_norm, w_uq, w_ukv, mla_q_norm, mla_k_norm, swa_q_norm,
                          swa_k_norm, sinks, out_norm, w_out)
        xt = _ffn(xt, mod, gains, l, 0, w_ffn1_gu[l], w_ffn1_down[l], seq=seq, tm=tm)
        mq, mk, mv, sq, sk, sv, bq, bk, bv = _prologue(xt, mod, gains, l, p, cos_t, sin_t, seq=seq, tm=tm)
        oa = _mla_attention(mq, mk, mv, batch=batch, seq=seq, blk=mla_blk)
        ob = _swa_attention(sq, sk, sv, positions, rel_bias, p["sinks"], batch=batch, seq=seq, tq=swa_tq)
        oc = _sb_attention(bq, bk, bv, batch=batch, seq=seq, blk=sb_blk)
        xt = _mix(xt, oa, ob, oc, mod, l, p, seq=seq, tm=tm)
        xt = _ffn(xt, mod, gains, l, 2, w_ffn2_gu[l], w_ffn2_down[l], seq=seq, tm=tm)
    return xt.reshape(batch, seq, d)
```

```python
import functools
import math

import numpy as np
import jax
import jax.numpy as jnp
from jax import lax
from jax.experimental import pallas as pl
from jax.experimental.pallas import tpu as pltpu

HEAD_DIM = 64
MLA_HEADS = 6
MLA_NOPE = 64
MLA_ROPE = 32
MLA_V = 64
MLA_QK = MLA_NOPE + MLA_ROPE
Q_LORA = 256
KV_LORA = 128
SWA_HEADS = 6
SWA_KV_HEADS = 2
SWA_GROUP = SWA_HEADS // SWA_KV_HEADS
WINDOW = 128
SB_HEADS = 4
NUM_BUCKETS = 32
MAX_DISTANCE = 128
ROPE_THETA = 10000.0
EPS = 1e-6

LANES = 128
SUBLANES = 8
VMEM_LIMIT_BYTES = 56 * 2**20

F32 = jnp.float32
BF16 = jnp.bfloat16
NEG_BIG = -1e30
LOG2E = math.log2(math.e)
SIGN_BIT = np.uint32(0x80000000)

PROJ_CQ = 0
PROJ_CKV = PROJ_CQ + Q_LORA
PROJ_KPE = PROJ_CKV + KV_LORA
PROJ_SWAQ = PROJ_KPE + LANES
PROJ_SWAK = PROJ_SWAQ + SWA_HEADS * HEAD_DIM
PROJ_SWAV = PROJ_SWAK + LANES
PROJ_SB = PROJ_SWAV + LANES
SB_WIDTH = SB_HEADS * HEAD_DIM
PROJ_KPESW = PROJ_SB + 3 * SB_WIDTH
PROJ_WIDTH = PROJ_KPESW + LANES
ROPE_HALF = MLA_ROPE // 2
MLA_SLABS = MLA_HEADS * LANES
MLA_VW = MLA_HEADS * MLA_V
SWA_W = SWA_HEADS * HEAD_DIM
SWA_HEAD_ORDER = (0, 3, 1, 4, 2, 5)


def _cparams():
    return pltpu.CompilerParams(vmem_limit_bytes=VMEM_LIMIT_BYTES)


def _resident(shape, index_map):
    return pl.BlockSpec(shape, index_map, pipeline_mode=pl.Buffered(1))


def _sigmoid(x):
    return 1.0 / (1.0 + jnp.exp(-x))


def _modulated_norm(x, gain, shift, scale):
    ms = jnp.mean(x * x, axis=-1, keepdims=True)
    y = x * lax.rsqrt(ms + EPS) * gain
    return y * (1.0 + scale) + shift


def _dot(a, b):
    return jnp.dot(a, b, preferred_element_type=F32)


def _dot_nt(a, b):
    return lax.dot_general(a, b, (((1,), (1,)), ((), ())), preferred_element_type=F32)


def _mod_kernel(c_ref, w_ref, b_ref, o_ref):
    c = c_ref[...]
    cond = c * _sigmoid(c)
    o_ref[0] = jnp.dot(cond, w_ref[0], preferred_element_type=F32,
                       precision=lax.Precision.HIGHEST) + b_ref[0]


def _modulation(c, w_mod, b_mod):
    depth, d, n = w_mod.shape
    b = c.shape[0]
    rows = SUBLANES
    c_pad = jnp.zeros((rows, d), F32).at[:b].set(c)
    bn = n // 8
    return pl.pallas_call(
        _mod_kernel,
        grid=(depth, n // bn),
        in_specs=[
            pl.BlockSpec((rows, d), lambda l, j: (0, 0)),
            pl.BlockSpec((1, d, bn), lambda l, j: (l, 0, j)),
            pl.BlockSpec((1, 1, bn), lambda l, j: (l, 0, j)),
        ],
        out_specs=pl.BlockSpec((1, rows, bn), lambda l, j: (l, 0, j)),
        out_shape=jax.ShapeDtypeStruct((depth, rows, n), F32),
        compiler_params=_cparams(),
        name="modulation",
    )(c_pad, w_mod, b_mod.reshape(depth, 1, n))


def _rope_kernel(pos_ref, freq_ref, cos_ref, sin_ref):
    ang = pos_ref[...].astype(F32) * freq_ref[...]
    lane = lax.broadcasted_iota(jnp.int32, ang.shape, 1)
    lo = MLA_NOPE
    mid = MLA_NOPE + MLA_ROPE // 2
    hi = MLA_NOPE + MLA_ROPE
    sn = jnp.sin(ang)
    cos_ref[...] = jnp.where(lane < lo, 1.0, jnp.where(lane < hi, jnp.cos(ang), 0.0))
    sin_ref[...] = jnp.where(lane < lo, 0.0, jnp.where(lane < mid, -sn, jnp.where(lane < hi, sn, 0.0)))


def _rope_tables(positions, tm):
    t = positions.size
    half = MLA_ROPE // 2
    freqs = ROPE_THETA ** (-jnp.arange(half, dtype=F32) / half)
    row = jnp.zeros((1, LANES), F32)
    row = row.at[0, MLA_NOPE:MLA_NOPE + half].set(freqs).at[0, MLA_NOPE + half:MLA_QK].set(freqs)
    return pl.pallas_call(
        _rope_kernel,
        grid=(t // tm,),
        in_specs=[pl.BlockSpec((tm, 1), lambda i: (i, 0)), pl.BlockSpec((1, LANES), lambda i: (0, 0))],
        out_specs=[pl.BlockSpec((tm, LANES), lambda i: (i, 0))] * 2,
        out_shape=[jax.ShapeDtypeStruct((t, LANES), F32)] * 2,
        compiler_params=_cparams(),
        name="rope_tables",
    )(positions.reshape(t, 1), row)


def _ffn_kernel(x_ref, sh_ref, sc_ref, gt_ref, g_ref, wgu_ref, wd_ref, o_ref, act_ref, *,
                tiles_per_batch, d_ff, chunk):
    b = pl.program_id(0) // tiles_per_batch
    x = x_ref[...]
    h = _modulated_norm(x, g_ref[0], sh_ref[0, pl.ds(b, 1), :], sc_ref[0, pl.ds(b, 1), :])
    hb = h.astype(BF16)
    for c in range(d_ff // chunk):
        g = _dot(hb, wgu_ref[:, c * chunk:(c + 1) * chunk])
        u = _dot(hb, wgu_ref[:, d_ff + c * chunk:d_ff + (c + 1) * chunk])
        act_ref[:, c * chunk:(c + 1) * chunk] = (g * _sigmoid(g) * u).astype(BF16)
    y = _dot(act_ref[...], wd_ref[...])
    o_ref[...] = x + (0.5 * gt_ref[0, pl.ds(b, 1), :]) * y


def _ffn(x, mod, gains, layer, which, w_gu, w_down, *, seq, tm):
    t, d = x.shape
    d_ff = w_down.shape[0]
    chunk = 2 * LANES
    rows = mod.shape[1]
    mod_spec = lambda k: pl.BlockSpec((1, rows, d), lambda i: (layer, 0, 3 * which + k))
    return pl.pallas_call(
        functools.partial(_ffn_kernel, tiles_per_batch=seq // tm, d_ff=d_ff, chunk=chunk),
        grid=(t // tm,),
        in_specs=[
            pl.BlockSpec((tm, d), lambda i: (i, 0)),
            mod_spec(0), mod_spec(1), mod_spec(2),
            pl.BlockSpec((1, 1, d), lambda i: (layer * 3 + which, 0, 0)),
            _resident((d, 2 * d_ff), lambda i: (0, 0)),
            _resident((d_ff, d), lambda i: (0, 0)),
        ],
        out_specs=pl.BlockSpec((tm, d), lambda i: (i, 0)),
        out_shape=jax.ShapeDtypeStruct((t, d), F32),
        scratch_shapes=[pltpu.VMEM((tm, d_ff), BF16)],
        compiler_params=_cparams(),
        name=f"ffn{which}",
    )(x, mod, mod, mod, gains, w_gu, w_down)


def _prologue_kernel(x_ref, sh_ref, sc_ref, g_ref, win_ref, qan_ref, kvan_ref, wuq_ref, wkn_ref, wv_ref,
                     qn_ref, qns_ref, kn_ref, kns_ref, sqn_ref, skn_ref, cos_ref, sin_ref,
                     mq_ref, mk_ref, mv_ref, sq_ref, sk_ref, sv_ref, bq_ref, bk_ref, bv_ref, *,
                     tiles_per_batch):
    b = pl.program_id(0) // tiles_per_batch
    x = x_ref[...]
    h = _modulated_norm(x, g_ref[0], sh_ref[0, pl.ds(b, 1), :], sc_ref[0, pl.ds(b, 1), :])
    proj = _dot(h.astype(BF16), win_ref[...])
    tm = x.shape[0]
    lane = lax.broadcasted_iota(jnp.int32, (tm, LANES), 1)
    low = lane < HEAD_DIM
    slabs = [slice(hd * LANES, (hd + 1) * LANES) for hd in range(MLA_HEADS)]

    def sumsq(v):
        return jnp.sum(v * v, axis=-1, keepdims=True)

    def inv_rms(ss, width):
        return lax.rsqrt(ss * (1.0 / width) + EPS)

    c_q = proj[:, PROJ_CQ:PROJ_CQ + Q_LORA]
    c_kv = proj[:, PROJ_CKV:PROJ_CKV + KV_LORA]
    ss_cq, ss_ckv = sumsq(c_q), sumsq(c_kv)
    cq = (c_q * inv_rms(ss_cq, Q_LORA) * qan_ref[...]).astype(BF16)
    ckv = (c_kv * inv_rms(ss_ckv, KV_LORA) * kvan_ref[...]).astype(BF16)
    q_both = _dot(cq, wuq_ref[...])
    k_pre = _dot(ckv, wkn_ref[...])
    mv_ref[...] = _dot(ckv, wv_ref[...]).astype(BF16)
    kpe = proj[:, PROJ_KPE:PROJ_KPE + LANES]
    kpe_sw = proj[:, PROJ_KPESW:PROJ_KPESW + LANES]
    cosv = cos_ref[...]
    sinv = sin_ref[...]
    q_cos = qn_ref[...] * cosv
    q_sin = qns_ref[...] * sinv
    k_cos = kn_ref[...] * cosv
    k_rot = kpe * k_cos + kpe_sw * (kns_ref[...] * sinv)
    swa = [proj[:, PROJ_SWAQ + j * LANES:PROJ_SWAQ + (j + 1) * LANES] for j in range(SWA_W // LANES)]
    swa.append(proj[:, PROJ_SWAK:PROJ_SWAK + LANES])

    kpe_ss = sumsq(kpe)
    q_ss = [sumsq(q_both[:, sl]) for sl in slabs]
    k_ss = [sumsq(k_pre[:, sl]) + kpe_ss for sl in slabs]
    swa_sq = [v * v for v in swa]
    swa_lo = [jnp.sum(jnp.where(low, sq, 0.0), axis=-1, keepdims=True) for sq in swa_sq]
    swa_hi = [jnp.sum(jnp.where(low, 0.0, sq), axis=-1, keepdims=True) for sq in swa_sq]
    q_scale = MLA_QK ** -0.5 * LOG2E
    q_r = [inv_rms(ss, MLA_QK) * q_scale for ss in q_ss]
    k_r = [inv_rms(ss, MLA_QK) for ss in k_ss]
    swa_r = [jnp.where(low, inv_rms(lo, HEAD_DIM), inv_rms(hi, HEAD_DIM)) for lo, hi in zip(swa_lo, swa_hi)]

    for hd, sl in enumerate(slabs):
        sw = slice(MLA_SLABS + hd * LANES, MLA_SLABS + (hd + 1) * LANES)
        mq_ref[:, sl] = ((q_both[:, sl] * q_cos + q_both[:, sw] * q_sin) * q_r[hd]).astype(BF16)
    for hd, sl in enumerate(slabs):
        mk_ref[:, sl] = ((k_pre[:, sl] * k_cos + k_rot) * k_r[hd]).astype(BF16)
    sqn = sqn_ref[...] * (HEAD_DIM ** -0.5)
    for j in range(SWA_W // LANES):
        sq_ref[:, j * LANES:(j + 1) * LANES] = (swa[j] * swa_r[j] * sqn).astype(BF16)
    sk_ref[...] = (swa[-1] * swa_r[-1] * skn_ref[...]).astype(BF16)
    sv_ref[...] = proj[:, PROJ_SWAV:PROJ_SWAV + LANES].astype(BF16)
    bq_ref[...] = (proj[:, PROJ_SB:PROJ_SB + SB_WIDTH] * (HEAD_DIM ** -0.5 * LOG2E)).astype(BF16)
    bk_ref[...] = proj[:, PROJ_SB + SB_WIDTH:PROJ_SB + 2 * SB_WIDTH].astype(BF16)
    bv_ref[...] = proj[:, PROJ_SB + 2 * SB_WIDTH:PROJ_SB + 3 * SB_WIDTH].astype(BF16)


def _prologue(x, mod, gains, layer, p, cos_t, sin_t, *, seq, tm):
    t, d = x.shape
    rows = mod.shape[1]
    mod_spec = lambda k: pl.BlockSpec((1, rows, d), lambda i: (layer, 0, 3 + k))
    const = lambda shape: _resident(shape, lambda i: (0,) * len(shape))
    tok = lambda w: pl.BlockSpec((tm, w), lambda i: (i, 0))
    widths = (MLA_SLABS, MLA_SLABS, MLA_VW, SWA_W, LANES, LANES, SB_WIDTH, SB_WIDTH, SB_WIDTH)
    return pl.pallas_call(
        functools.partial(_prologue_kernel, tiles_per_batch=seq // tm),
        grid=(t // tm,),
        in_specs=[
            tok(d), mod_spec(0), mod_spec(1),
            pl.BlockSpec((1, 1, d), lambda i: (layer * 3 + 1, 0, 0)),
            const((d, PROJ_WIDTH)), const((1, Q_LORA)), const((1, KV_LORA)),
            const((Q_LORA, 2 * MLA_SLABS)), const((KV_LORA, MLA_SLABS)), const((KV_LORA, MLA_VW)),
            const((1, LANES)), const((1, LANES)), const((1, LANES)), const((1, LANES)),
            const((1, LANES)), const((1, LANES)),
            tok(LANES), tok(LANES),
        ],
        out_specs=[tok(w) for w in widths],
        out_shape=[jax.ShapeDtypeStruct((t, w), BF16) for w in widths],
        compiler_params=_cparams(),
        name="attn_prologue",
    )(x, mod, mod, gains, p["w_in"], p["q_a_norm"], p["kv_a_norm"], p["w_uq"], p["w_kn"], p["w_v"],
      p["mla_q_norm"], p["mla_q_norm_sw"], p["mla_k_norm"], p["mla_k_norm_sw"], p["swa_q_norm"], p["swa_k_norm"],
      cos_t, sin_t)


def _mla_kernel(q_ref, k_ref, v_ref, o_ref, s_scr, pmax_scr, p_scr, acc_scr, m_scr, l_scr, alpha_scr, *, blk):
    i = pl.program_id(2)
    lane = lax.broadcasted_iota(jnp.int32, (blk, LANES), 1)
    row = lax.broadcasted_iota(jnp.int32, (blk, blk), 0)
    col = lax.broadcasted_iota(jnp.int32, (blk, blk), 1)
    causal = col <= row
    qs = (q_ref[:, 0:LANES], q_ref[:, LANES:2 * LANES])
    heads = range(2)

    def rows(ref, j):
        return ref[pl.ds(pl.multiple_of(j * blk, blk), blk), :]

    chunks = [slice(c * LANES, (c + 1) * LANES) for c in range(blk // LANES)]

    def scores(j, slot, masked):
        kj = rows(k_ref, j)
        for hd in heads:
            s = _dot_nt(qs[hd], kj[:, hd * LANES:(hd + 1) * LANES])
            if masked:
                s = jnp.where(causal, s, NEG_BIG)
            s_scr[slot, hd] = s
            pm = s[:, chunks[0]]
            for sl in chunks[1:]:
                pm = jnp.maximum(pm, s[:, sl])
            pmax_scr[slot, hd] = pm

    def accumulate(j, slot):
        vj = rows(v_ref, j)
        return tuple(alpha_scr[hd] * acc_scr[hd] + _dot(p_scr[slot, hd], vj) for hd in heads)

    def softmax(src, dst):
        m_new = [jnp.maximum(m_scr[hd], jnp.max(pmax_scr[src, hd], axis=-1, keepdims=True)) for hd in heads]
        for hd in heads:
            psum = None
            for sl in chunks:
                p = jnp.exp2(s_scr[src, hd, :, sl] - m_new[hd])
                p_scr[dst, hd, :, sl] = p.astype(BF16)
                psum = p if psum is None else psum + p
            alpha = jnp.exp2(m_scr[hd] - m_new[hd])
            l_scr[hd] = alpha * l_scr[hd] + jnp.sum(psum, axis=-1, keepdims=True)
            alpha_scr[hd] = alpha
            m_scr[hd] = m_new[hd]

    def step(b, src, dst):
        scores(jnp.maximum(b - 1, 0), dst, False)
        accs = accumulate(b + 1, src)
        for hd in heads:
            acc_scr[hd] = accs[hd]
        softmax(src, dst)

    scores(i, 0, True)
    scores(jnp.maximum(i - 1, 0), 1, False)
    acc_scr[...] = jnp.zeros_like(acc_scr)
    l_scr[...] = jnp.zeros_like(l_scr)
    m_scr[...] = jnp.full(m_scr.shape, NEG_BIG, F32)
    softmax(0, 1)

    def body(t, carry):
        b = i - 1 - 2 * t
        step(b, 1, 0)
        step(b - 1, 0, 1)
        return carry

    lax.fori_loop(0, i // 2, body, 0)

    def finish(slot):
        a0, a1 = accumulate(0, slot)
        o_ref[...] = jnp.where(lane < MLA_V, a0 * (1.0 / l_scr[0]), a1 * (1.0 / l_scr[1])).astype(BF16)

    @pl.when(i % 2 == 1)
    def _():
        step(0, 1, 0)
        finish(0)

    @pl.when(i % 2 == 0)
    def _():
        finish(1)


def _mla_attention(q, k, v, *, batch, seq, blk):
    t = q.shape[0]
    nq = seq // blk
    pairs = MLA_HEADS // 2
    return pl.pallas_call(
        functools.partial(_mla_kernel, blk=blk),
        grid=(batch, pairs, nq),
        in_specs=[
            pl.BlockSpec((blk, 2 * LANES), lambda b, p, i: (b * nq + i, p)),
            pl.BlockSpec((seq, 2 * LANES), lambda b, p, i: (b, p)),
            pl.BlockSpec((seq, LANES), lambda b, p, i: (b, p)),
        ],
        out_specs=pl.BlockSpec((blk, LANES), lambda b, p, i: (b * nq + i, p)),
        out_shape=jax.ShapeDtypeStruct((t, MLA_VW), BF16),
        scratch_shapes=[
            pltpu.VMEM((2, 2, blk, blk), F32),
            pltpu.VMEM((2, 2, blk, LANES), F32),
            pltpu.VMEM((2, 2, blk, blk), BF16),
            pltpu.VMEM((2, blk, LANES), F32),
            pltpu.VMEM((2, blk, LANES), F32),
            pltpu.VMEM((2, blk, LANES), F32),
            pltpu.VMEM((2, blk, LANES), F32),
        ],
        compiler_params=_cparams(),
        name="mla_attention",
    )(q, k, v)


def _t5_bucket(rel):
    n = jnp.maximum(rel, 0)
    max_exact = NUM_BUCKETS // 2
    nf = jnp.maximum(n, 1).astype(F32)
    large = max_exact + (jnp.log(nf / max_exact) / math.log(MAX_DISTANCE / max_exact)
                         * (NUM_BUCKETS - max_exact)).astype(jnp.int32)
    large = jnp.minimum(large, NUM_BUCKETS - 1)
    return jnp.where(n < max_exact, n, large)


def _swa_kernel(q_ref, k_ref, v_ref, posq_ref, posk_ref, tab_ref, sink_ref, o_ref, *, tq):
    i = pl.program_id(1)
    w = WINDOW
    dist = lax.broadcasted_iota(jnp.int32, (SUBLANES, LANES), 1)
    bucket = _t5_bucket(dist)
    luts = []
    for hd in range(SWA_HEADS):
        lut = jnp.zeros((SUBLANES, LANES), F32)
        for kk in range(NUM_BUCKETS):
            lut = jnp.where(bucket == kk, tab_ref[kk, hd], lut)
        luts.append(jnp.broadcast_to(lut[0:1, :], (w, LANES)))
    lane = lax.broadcasted_iota(jnp.int32, (w, LANES), 1)
    low = lane < HEAD_DIM
    qrow = lax.broadcasted_iota(jnp.int32, (w, 2 * w), 0)
    kcol = lax.broadcasted_iota(jnp.int32, (w, 2 * w), 1)
    nblk = tq // w
    nslab = SWA_W // LANES
    chunk_heads = [SWA_HEAD_ORDER[2 * j + half] for half in range(2) for j in range(nslab)]
    blocks = range(nblk)

    bands, scores = [], []
    for blk in blocks:
        q0 = (i * nblk + blk) * w
        ks = pl.multiple_of(jnp.maximum(q0 - w, 0), w)
        kb = k_ref[pl.ds(ks, 2 * w), :]
        vb = v_ref[pl.ds(ks, 2 * w), :]
        rel = posq_ref[blk * w:(blk + 1) * w, :] - posk_ref[0, :, pl.ds(ks, 2 * w)]
        idx = jnp.clip(rel, 0, LANES - 1)
        d = (q0 + qrow) - (ks + kcol)
        bands.append((vb, idx, (d >= 0) & (d < w)))
        qs = [q_ref[blk * w:(blk + 1) * w, j * LANES:(j + 1) * LANES] for j in range(nslab)]
        stacked = jnp.concatenate([jnp.where(low, q, jnp.zeros_like(q)) for q in qs]
                                  + [jnp.where(low, jnp.zeros_like(q), q) for q in qs], axis=0)
        scores.append(_dot_nt(stacked, kb))
    logits = []
    for blk in blocks:
        _, idx, valid = bands[blk]
        per_head = []
        for r, hd in enumerate(chunk_heads):
            bias = jnp.concatenate(
                [jnp.take_along_axis(luts[hd], idx[:, :LANES], axis=1),
                 jnp.take_along_axis(luts[hd], idx[:, LANES:], axis=1)], axis=1)
            per_head.append(jnp.where(valid, scores[blk][r * w:(r + 1) * w] + bias, NEG_BIG))
        logits.append(per_head)
    maxes = [[jnp.maximum(jnp.max(s, axis=-1, keepdims=True), sink_ref[hd])
              for s, hd in zip(logits[blk], chunk_heads)] for blk in blocks]
    probs = [[jnp.exp(s - m) for s, m in zip(logits[blk], maxes[blk])] for blk in blocks]
    dens = [[jnp.sum(p, axis=-1, keepdims=True) + jnp.exp(sink_ref[hd] - m)
             for p, m, hd in zip(probs[blk], maxes[blk], chunk_heads)] for blk in blocks]
    for blk in blocks:
        pn = jnp.concatenate([(p * (1.0 / den)).astype(BF16) for p, den in zip(probs[blk], dens[blk])], axis=0)
        o = _dot(pn, bands[blk][0])
        for j in range(nslab):
            o_ref[blk * w:(blk + 1) * w, j * LANES:(j + 1) * LANES] = jnp.where(
                low, o[j * w:(j + 1) * w], o[(nslab + j) * w:(nslab + j + 1) * w]).astype(BF16)


def _swa_attention(q, k, v, positions, rel_bias, sinks, *, batch, seq, tq):
    t = q.shape[0]
    nq = seq // tq
    return pl.pallas_call(
        functools.partial(_swa_kernel, tq=tq),
        grid=(batch, nq),
        in_specs=[
            pl.BlockSpec((tq, SWA_W), lambda b, i: (b * nq + i, 0)),
            pl.BlockSpec((seq, LANES), lambda b, i: (b, 0)),
            pl.BlockSpec((seq, LANES), lambda b, i: (b, 0)),
            pl.BlockSpec((tq, 1), lambda b, i: (b * nq + i, 0)),
            pl.BlockSpec((1, 1, seq), lambda b, i: (b, 0, 0)),
            pl.BlockSpec(memory_space=pltpu.SMEM),
            pl.BlockSpec(memory_space=pltpu.SMEM),
        ],
        out_specs=pl.BlockSpec((tq, SWA_W), lambda b, i: (b * nq + i, 0)),
        out_shape=jax.ShapeDtypeStruct((t, SWA_W), BF16),
        compiler_params=_cparams(),
        name="swa_attention",
    )(q, k, v, positions.reshape(t, 1), positions.reshape(batch, 1, seq), rel_bias, sinks)


def _sb_kernel(q_ref, k_ref, v_ref, u_ref, o_ref, z_scr, p_scr, acc_scr, c_scr, *, blk):
    i = pl.program_id(2)
    lane = lax.broadcasted_iota(jnp.int32, (blk, LANES), 1)
    low = lane < HEAD_DIM
    row = lax.broadcasted_iota(jnp.int32, (blk, blk), 0)
    col = lax.broadcasted_iota(jnp.int32, (blk, blk), 1)
    strict = col < row
    q = q_ref[...]
    zero = jnp.zeros_like(q)
    qs = (jnp.where(low, q, zero), jnp.where(low, zero, q))
    heads = range(2)

    def rows(ref, j):
        return ref[pl.ds(pl.multiple_of(j * blk, blk), blk), :]

    def scores(j, slot):
        kj = rows(k_ref, j)
        for hd in heads:
            z_scr[slot, hd] = _dot_nt(qs[hd], kj)

    def accumulate(j, slot):
        vj = rows(v_ref, j)
        return tuple(acc_scr[hd] + _dot(p_scr[slot, hd], vj) for hd in heads)

    def weights(src, dst, masked):
        sps = []
        for hd in heads:
            z = z_scr[src, hd]
            neg_abs = lax.bitcast_convert_type(lax.bitcast_convert_type(z, jnp.uint32) | SIGN_BIT, F32)
            sp = jnp.maximum(z, 0.0) + jnp.log(1.0 + jnp.exp2(neg_abs)) * LOG2E
            if masked:
                sp = jnp.where(strict, sp, 0.0)
            sps.append(sp)
        tri = u_ref[...]
        chunks = [slice(c * LANES, (c + 1) * LANES) for c in range(blk // LANES)]
        for hd in heads:
            later = _dot(sps[hd].astype(BF16), tri)
            for sl in chunks:
                a = jnp.exp2((z_scr[src, hd, :, sl] - sps[hd][:, sl]) - (later[:, sl] + c_scr[hd]))
                if masked:
                    a = jnp.where(strict[:, sl], a, 0.0)
                p_scr[dst, hd, :, sl] = a.astype(BF16)
        for hd in heads:
            folded = sps[hd][:, chunks[0]]
            for sl in chunks[1:]:
                folded = folded + sps[hd][:, sl]
            c_scr[hd] += jnp.sum(folded, axis=-1, keepdims=True)

    def step(b, src, dst):
        scores(jnp.maximum(b - 1, 0), dst)
        accs = accumulate(b + 1, src)
        for hd in heads:
            acc_scr[hd] = accs[hd]
        weights(src, dst, False)

    scores(i, 0)
    scores(jnp.maximum(i - 1, 0), 1)
    acc_scr[...] = jnp.zeros_like(acc_scr)
    c_scr[...] = jnp.zeros_like(c_scr)
    weights(0, 1, True)

    def body(t, carry):
        b = i - 1 - 2 * t
        step(b, 1, 0)
        step(b - 1, 0, 1)
        return carry

    lax.fori_loop(0, i // 2, body, 0)

    def finish(slot):
        a0, a1 = accumulate(0, slot)
        o_ref[...] = jnp.where(low, a0, a1).astype(BF16)

    @pl.when(i % 2 == 1)
    def _():
        step(0, 1, 0)
        finish(0)

    @pl.when(i % 2 == 0)
    def _():
        finish(1)


def _sb_attention(q, k, v, *, batch, seq, blk):
    t = q.shape[0]
    nq = seq // blk
    pairs = SB_HEADS // 2
    tri = jnp.tril(jnp.ones((blk, blk), BF16), k=-1)
    return pl.pallas_call(
        functools.partial(_sb_kernel, blk=blk),
        grid=(batch, pairs, nq),
        in_specs=[
            pl.BlockSpec((blk, LANES), lambda b, p, i: (b * nq + i, p)),
            pl.BlockSpec((seq, LANES), lambda b, p, i: (b, p)),
            pl.BlockSpec((seq, LANES), lambda b, p, i: (b, p)),
            _resident((blk, blk), lambda b, p, i: (0, 0)),
        ],
        out_specs=pl.BlockSpec((blk, LANES), lambda b, p, i: (b * nq + i, p)),
        out_shape=jax.ShapeDtypeStruct((t, SB_WIDTH), BF16),
        scratch_shapes=[
            pltpu.VMEM((2, 2, blk, blk), F32),
            pltpu.VMEM((2, 2, blk, blk), BF16),
            pltpu.VMEM((2, blk, LANES), F32),
            pltpu.VMEM((2, blk, LANES), F32),
        ],
        compiler_params=_cparams(),
        name="sb_attention",
    )(q, k, v, tri)


def _mix_kernel(x_ref, oa_ref, ob_ref, oc_ref, gt_ref, na_ref, nb_ref, nc_ref, wa_ref, wb_ref, wc_ref, o_ref, *,
                tiles_per_batch):
    b = pl.program_id(0) // tiles_per_batch

    def group(o_ref_, n_ref_):
        o = o_ref_[...].astype(F32)
        return (o * lax.rsqrt(jnp.mean(o * o, axis=-1, keepdims=True) + EPS) * n_ref_[...]).astype(BF16)

    y = (_dot(group(oa_ref, na_ref), wa_ref[...]) + _dot(group(ob_ref, nb_ref), wb_ref[...])
         + _dot(group(oc_ref, nc_ref), wc_ref[...]))
    o_ref[...] = x_ref[...] + gt_ref[0, pl.ds(b, 1), :] * y


def _mix(x, oa, ob, oc, mod, layer, p, *, seq, tm):
    t, d = x.shape
    rows = mod.shape[1]
    const = lambda shape: _resident(shape, lambda i: (0,) * len(shape))
    tok = lambda w: pl.BlockSpec((tm, w), lambda i: (i, 0))
    return pl.pallas_call(
        functools.partial(_mix_kernel, tiles_per_batch=seq // tm),
        grid=(t // tm,),
        in_specs=[
            tok(d), tok(MLA_VW), tok(SWA_W), tok(SB_WIDTH),
            pl.BlockSpec((1, rows, d), lambda i: (layer, 0, 5)),
            const((1, MLA_VW)), const((1, SWA_W)), const((1, SB_WIDTH)),
            const((MLA_VW, d)), const((SWA_W, d)), const((SB_WIDTH, d)),
        ],
        out_specs=tok(d),
        out_shape=jax.ShapeDtypeStruct((t, d), F32),
        compiler_params=_cparams(),
        name="mix_out_proj",
    )(x, oa, ob, oc, mod, p["on_a"], p["on_b"], p["on_c"], p["w_out_a"], p["w_out_b"], p["w_out_c"])


def _proj_columns():
    idx = np.full((PROJ_WIDTH,), -1, np.int64)
    o_ckv = Q_LORA
    o_kpe = o_ckv + KV_LORA
    o_qs = o_kpe + MLA_ROPE
    o_ks = o_qs + SWA_HEADS * HEAD_DIM
    o_vs = o_ks + SWA_KV_HEADS * HEAD_DIM
    o_sb = o_vs + SWA_KV_HEADS * HEAD_DIM
    idx[PROJ_CQ:PROJ_CQ + Q_LORA] = np.arange(Q_LORA)
    idx[PROJ_CKV:PROJ_CKV + KV_LORA] = o_ckv + np.arange(KV_LORA)
    idx[PROJ_KPE + MLA_NOPE:PROJ_KPE + MLA_QK] = o_kpe + np.arange(MLA_ROPE)
    for slot, hd in enumerate(SWA_HEAD_ORDER):
        idx[PROJ_SWAQ + slot * HEAD_DIM:PROJ_SWAQ + (slot + 1) * HEAD_DIM] = o_qs + hd * HEAD_DIM + np.arange(HEAD_DIM)
    idx[PROJ_SWAK:PROJ_SWAK + LANES] = o_ks + np.arange(LANES)
    idx[PROJ_SWAV:PROJ_SWAV + LANES] = o_vs + np.arange(LANES)
    idx[PROJ_SB:PROJ_SB + 3 * SB_WIDTH] = o_sb + np.arange(3 * SB_WIDTH)
    idx[PROJ_KPESW + MLA_NOPE:PROJ_KPESW + MLA_QK] = o_kpe + _swap_halves(np.arange(MLA_ROPE))
    return idx


def _swap_halves(v):
    return np.concatenate([v[ROPE_HALF:], v[:ROPE_HALF]])


def _take_columns(w, idx):
    w_ext = jnp.concatenate([w, jnp.zeros((w.shape[0], 1), w.dtype)], axis=1)
    return w_ext[:, np.where(idx < 0, w.shape[1], idx)]


def _layer_params(l, w_in, q_a_norm, kv_a_norm, w_uq, w_ukv, mla_q_norm, mla_k_norm, swa_q_norm, swa_k_norm,
                  sinks, out_norm, w_out):
    uq_idx = np.full((2 * MLA_SLABS,), -1, np.int64)
    kn_idx = np.full((MLA_SLABS,), -1, np.int64)
    v_idx = np.zeros((MLA_VW,), np.int64)
    rope_sw = MLA_NOPE + _swap_halves(np.arange(MLA_ROPE))
    for hd in range(MLA_HEADS):
        uq_idx[hd * LANES:hd * LANES + MLA_QK] = hd * MLA_QK + np.arange(MLA_QK)
        uq_idx[MLA_SLABS + hd * LANES + MLA_NOPE:MLA_SLABS + hd * LANES + MLA_QK] = hd * MLA_QK + rope_sw
        kn_idx[hd * LANES:hd * LANES + MLA_NOPE] = hd * (MLA_NOPE + MLA_V) + np.arange(MLA_NOPE)
        v_idx[hd * MLA_V:(hd + 1) * MLA_V] = hd * (MLA_NOPE + MLA_V) + MLA_NOPE + np.arange(MLA_V)
    pad = lambda g: jnp.zeros((1, LANES), F32).at[0, :MLA_QK].set(g)
    pad_sw = lambda g: jnp.zeros((1, LANES), F32).at[0, MLA_NOPE:MLA_QK].set(g[rope_sw])
    pair = lambda g: jnp.concatenate([g, g]).reshape(1, LANES)
    swa_rows = np.concatenate([MLA_VW + hd * HEAD_DIM + np.arange(HEAD_DIM) for hd in SWA_HEAD_ORDER])
    return {
        "w_in": _take_columns(w_in[l], _proj_columns()).astype(BF16),
        "q_a_norm": q_a_norm[l].reshape(1, Q_LORA),
        "kv_a_norm": kv_a_norm[l].reshape(1, KV_LORA),
        "w_uq": _take_columns(w_uq[l], uq_idx).astype(BF16),
        "w_kn": _take_columns(w_ukv[l], kn_idx).astype(BF16),
        "w_v": w_ukv[l][:, v_idx].astype(BF16),
        "mla_q_norm": pad(mla_q_norm[l]),
        "mla_q_norm_sw": pad_sw(mla_q_norm[l]),
        "mla_k_norm": pad(mla_k_norm[l]),
        "mla_k_norm_sw": pad_sw(mla_k_norm[l]),
        "swa_q_norm": pair(swa_q_norm[l]),
        "swa_k_norm": pair(swa_k_norm[l]),
        "sinks": sinks[l],
        "on_a": out_norm[l][:MLA_VW].reshape(1, MLA_VW),
        "on_b": out_norm[l][swa_rows].reshape(1, SWA_W),
        "on_c": out_norm[l][MLA_VW + SWA_W:].reshape(1, SB_WIDTH),
        "w_out_a": w_out[l][:MLA_VW].astype(BF16),
        "w_out_b": w_out[l][swa_rows].astype(BF16),
        "w_out_c": w_out[l][MLA_VW + SWA_W:].astype(BF16),
    }


def kernel(x, c, positions, rel_bias, norm_g, w_mod, b_mod, w_ffn1_gu, w_ffn1_down, w_in, q_a_norm, kv_a_norm,
           w_uq, w_ukv, mla_q_norm, mla_k_norm, swa_q_norm, swa_k_norm, sinks, out_norm, w_out, w_ffn2_gu,
           w_ffn2_down):
    batch, seq, d = x.shape
    depth = w_mod.shape[0]
    t = batch * seq
    tm = min(512, seq)
    mla_blk = min(512, seq)
    sb_blk = min(256, seq)
    swa_tq = min(512, seq)

    xt = x.reshape(t, d)
    mod = _modulation(c, w_mod, b_mod)
    gains = norm_g.reshape(depth * 3, 1, d)
    cos_t, sin_t = _rope_tables(positions, tm)
    ffn_w = [w.astype(BF16) for w in (w_ffn1_gu, w_ffn1_down, w_ffn2_gu, w_ffn2_down)]
    for l in range(depth):
        p = _layer_params(l, w_in, q_a_norm, kv_a_norm, w_uq, w_ukv, mla_q_norm, mla_k_norm, swa_q_norm,
                          swa_k_norm, sinks, out_norm, w_out)
        xt = _ffn(xt, mod, gains, l, 0, ffn_w[0][l], ffn_w[1][l], seq=seq, tm=tm)
        mq, mk, mv, sq, sk, sv, bq, bk, bv = _prologue(xt, mod, gains, l, p, cos_t, sin_t, seq=seq, tm=tm)
        oa = _mla_attention(mq, mk, mv, batch=batch, seq=seq, blk=mla_blk)
        ob = _swa_attention(sq, sk, sv, positions, rel_bias, p["sinks"], batch=batch, seq=seq, tq=swa_tq)
        oc = _sb_attention(bq, bk, bv, batch=batch, seq=seq, blk=sb_blk)
        xt = _mix(xt, oa, ob, oc, mod, l, p, seq=seq, tm=tm)
        xt = _ffn(xt, mod, gains, l, 2, ffn_w[2][l], ffn_w[3][l], seq=seq, tm=tm)
    return xt.reshape(batch, seq, d)
```
